```python
import jax, jax.numpy as jnp
from jax import lax
import numpy as np

D_MODEL = 2048
BATCH = 1
SEQ = 16384
DEPTH = 2

EPS = 1e-6
NEG_INF = -1e30
BLOCK = 128
HEAD_DIM = 64
ROPE_DIM = HEAD_DIM // 4
ROPE_THETA = 500000.0
A_DILATED = ((128, 1), (512, 4), (2048, 16))
A_HEADS_PER_GROUP = 6
A_HEADS = len(A_DILATED) * A_HEADS_PER_GROUP
A_WIDTH = A_HEADS * HEAD_DIM
A_OUT = A_HEADS_PER_GROUP * HEAD_DIM
B_WINDOW = 128
B_Q_HEADS = 16
B_KV_HEADS = 4
B_REP = B_Q_HEADS // B_KV_HEADS
B_Q_WIDTH = B_Q_HEADS * HEAD_DIM
B_KV_WIDTH = B_KV_HEADS * HEAD_DIM
C_HEADS = 8
C_HEAD_DIM = 128
C_WIDTH = C_HEADS * C_HEAD_DIM
C_CHUNK = 128
C_ROT_THETA = 10000.0
D_FF = 4 * D_MODEL
IN_SIZES = (A_WIDTH, A_WIDTH, A_WIDTH,
            B_Q_WIDTH, B_KV_WIDTH, B_KV_WIDTH,
            C_WIDTH, C_WIDTH, C_WIDTH, C_WIDTH,
            D_MODEL, D_MODEL, D_MODEL)
D_IN = sum(IN_SIZES)

kernel_name = "hybrid_dilated_swa_sink_retention_gated_block"


def rms_norm(x, g):
    xf = x.astype(jnp.float32)
    y = xf * lax.rsqrt(jnp.mean(xf * xf, axis=-1, keepdims=True) + EPS)
    return (y * g.astype(jnp.float32)).astype(x.dtype)


def rotate(x, pos, rot_dim, theta):
    half = rot_dim // 2
    inv = theta ** (-jnp.arange(half, dtype=jnp.float32) / half)
    ang = pos.astype(jnp.float32)[:, None] * inv[None, :]
    cos = jnp.cos(ang)[None, :, None, :]
    sin = jnp.sin(ang)[None, :, None, :]
    xr = x[..., :rot_dim].astype(jnp.float32)
    x1, x2 = xr[..., :half], xr[..., half:]
    rot = jnp.concatenate([x1 * cos - x2 * sin, x2 * cos + x1 * sin], axis=-1).astype(x.dtype)
    return jnp.concatenate([rot, x[..., rot_dim:]], axis=-1)


def split_columns(p):
    outs, start = [], 0
    for size in IN_SIZES:
        outs.append(p[..., start:start + size])
        start += size
    return outs


def banded_attention(q, k, v, max_dist, sinks=None):
    n, L, g, r, dh = q.shape
    nb = -(-L // BLOCK)
    pad = nb * BLOCK - L
    if pad:
        q = jnp.pad(q, ((0, 0), (0, pad), (0, 0), (0, 0), (0, 0)))
        k = jnp.pad(k, ((0, 0), (0, pad), (0, 0), (0, 0)))
        v = jnp.pad(v, ((0, 0), (0, pad), (0, 0), (0, 0)))
    qb = q.reshape(n, nb, BLOCK, g, r, dh)
    kb = k.reshape(n, nb, BLOCK, g, dh)
    vb = v.reshape(n, nb, BLOCK, g, dh)

    def with_prev(t):
        prev = jnp.pad(t[:, :-1], ((0, 0), (1, 0), (0, 0), (0, 0), (0, 0)))
        return jnp.concatenate([prev, t], axis=2)

    kk, vv = with_prev(kb), with_prev(vb)
    s = jnp.einsum('nbqgrd,nbkgd->nbgrqk', qb, kk,
                   preferred_element_type=jnp.float32) * (dh ** -0.5)
    qi = jnp.arange(BLOCK)[:, None]
    kj = jnp.arange(2 * BLOCK)[None, :]
    dist = qi + BLOCK - kj
    band = (dist >= 0) & (dist <= max_dist)
    has_prev = (jnp.arange(nb) > 0)[:, None, None] | (kj >= BLOCK)[None]
    valid = band[None] & has_prev
    s = jnp.where(valid[None, :, None, None], s, NEG_INF)
    m = jnp.max(s, axis=-1, keepdims=True)
    if sinks is not None:
        sk = sinks.astype(jnp.float32).reshape(1, 1, g, r, 1, 1)
        m = jnp.maximum(m, sk)
    p = jnp.exp(s - m)
    denom = jnp.sum(p, axis=-1, keepdims=True)
    if sinks is not None:
        denom = denom + jnp.exp(sk - m)
    o = jnp.einsum('nbgrqk,nbkgd->nbqgrd', p, vv.astype(jnp.float32))
    denom_q = jnp.moveaxis(denom[..., 0], -1, 2)
    o = (o / denom_q[..., None]).reshape(n, nb * BLOCK, g, r, dh)[:, :L]
    lse = jnp.moveaxis((m + jnp.log(denom))[..., 0], -1, 2)
    lse = lse.reshape(n, nb * BLOCK, g, r)[:, :L]
    return o.astype(q.dtype), lse


def dilated_attention(q, k, v):
    b, s = q.shape[:2]
    hpg = A_HEADS_PER_GROUP
    outs, lses = [], []
    for gi, (window, dil) in enumerate(A_DILATED):
        sl = slice(gi * hpg, (gi + 1) * hpg)

        def strided(t):
            t = t.reshape(b, s // dil, dil, hpg, HEAD_DIM).transpose(0, 2, 1, 3, 4)
            return t.reshape(b * dil, s // dil, hpg, HEAD_DIM)

        o, lse = banded_attention(strided(q[:, :, sl])[:, :, :, None],
                                  strided(k[:, :, sl]), strided(v[:, :, sl]), window // dil)
        o = o[:, :, :, 0].reshape(b, dil, s // dil, hpg, HEAD_DIM).transpose(0, 2, 1, 3, 4)
        lse = lse[:, :, :, 0].reshape(b, dil, s // dil, hpg).transpose(0, 2, 1, 3)
        outs.append(o.reshape(b, s, hpg, HEAD_DIM))
        lses.append(lse.reshape(b, s, hpg))
    w = jax.nn.softmax(jnp.stack(lses, axis=0), axis=0)
    o = jnp.sum(w[..., None] * jnp.stack(outs, axis=0).astype(jnp.float32), axis=0)
    return o.astype(q.dtype)


def retention(q, k, v, pos):
    b, s, h, dk = q.shape
    dv = v.shape[-1]
    q = rotate(q, pos, dk, C_ROT_THETA)
    k = rotate(k, pos, dk, C_ROT_THETA) * (dk ** -0.5)
    log_g = jnp.log1p(-(2.0 ** (-5.0 - jnp.arange(h, dtype=jnp.float32))))
    n, c = s // C_CHUNK, C_CHUNK
    qc = q.reshape(b, n, c, h, dk)
    kc = k.reshape(b, n, c, h, dk)
    vc = v.reshape(b, n, c, h, dv).astype(jnp.float32)
    i = jnp.arange(c, dtype=jnp.float32)
    rel = i[:, None] - i[None, :]
    decay = jnp.where(rel >= 0, jnp.exp(log_g[:, None, None] * jnp.maximum(rel, 0.0)), 0.0)
    inner = jnp.einsum('bnihd,bnjhd->bnhij', qc, kc, preferred_element_type=jnp.float32) * decay
    inner = jnp.einsum('bnhij,bnjhe->bnihe', inner, vc)
    k_decay = jnp.exp(log_g[:, None] * (c - 1 - i)[None, :])
    kv = jnp.einsum('bnjhd,hj,bnjhe->bnhde', kc.astype(jnp.float32), k_decay, vc)
    chunk_decay = jnp.exp(log_g * c)[None, :, None, None]

    def step(state, kv_n):
        return chunk_decay * state + kv_n, state

    _, prev = lax.scan(step, jnp.zeros((b, h, dk, dv), jnp.float32), jnp.moveaxis(kv, 1, 0))
    prev = jnp.moveaxis(prev, 0, 1)
    q_decay = jnp.exp(log_g[None, :] * (i + 1.0)[:, None])
    cross = jnp.einsum('bnihd,bnhde->bnihe', qc.astype(jnp.float32), prev) * q_decay[None, None, :, :, None]
    return (inner + cross).reshape(b, s, h, dv)


def setup_inputs(seed: int = 0) -> dict:
    key = jax.random.key(seed)
    ks = jax.random.split(key, 18)
    f32 = jnp.float32

    def nrm(k, shape, scale):
        return jax.random.normal(k, shape, f32) * scale

    def gain(k, shape):
        return 1.0 + 0.02 * jax.random.normal(k, shape, f32)

    return {
        "x": nrm(ks[0], (BATCH, SEQ, D_MODEL), 1.0),
        "mix_norm": gain(ks[1], (DEPTH, D_MODEL)),
        "w_in": nrm(ks[2], (DEPTH, D_MODEL, D_IN), D_MODEL ** -0.5),
        "a_q_norm": gain(ks[3], (DEPTH, HEAD_DIM)),
        "a_k_norm": gain(ks[4], (DEPTH, HEAD_DIM)),
        "b_q_norm": gain(ks[5], (DEPTH, HEAD_DIM)),
        "b_k_norm": gain(ks[6], (DEPTH, HEAD_DIM)),
        "b_sinks": nrm(ks[7], (DEPTH, B_Q_HEADS), 0.5),
        "c_gn": gain(ks[8], (DEPTH, C_WIDTH)),
        "w_br_a": nrm(ks[9], (DEPTH, A_OUT, D_MODEL), A_OUT ** -0.5),
        "w_br_b": nrm(ks[10], (DEPTH, B_Q_WIDTH, D_MODEL), B_Q_WIDTH ** -0.5),
        "w_br_c": nrm(ks[11], (DEPTH, C_WIDTH, D_MODEL), C_WIDTH ** -0.5),
        "w_out": nrm(ks[12], (DEPTH, D_MODEL, D_MODEL), D_MODEL ** -0.5),
        "mlp_norm": gain(ks[13], (DEPTH, D_MODEL)),
        "w_up": nrm(ks[14], (DEPTH, D_MODEL, D_FF), D_MODEL ** -0.5),
        "w_down": nrm(ks[15], (DEPTH, D_FF, D_MODEL), D_FF ** -0.5),
    }


def reference(x, mix_norm, w_in, a_q_norm, a_k_norm, b_q_norm, b_k_norm, b_sinks, c_gn,
              w_br_a, w_br_b, w_br_c, w_out, mlp_norm, w_up, w_down):
    b, s, _ = x.shape
    pos = jnp.arange(s)
    for l in range(DEPTH):
        u = rms_norm(x, mix_norm[l])
        proj = jnp.einsum('bsd,de->bse', u, w_in[l])
        aq, ak, av, bq, bk, bv, cq, ck, cv, cg, ga, gb, gc = split_columns(proj)

        aq = rotate(rms_norm(aq.reshape(b, s, A_HEADS, HEAD_DIM), a_q_norm[l]), pos, ROPE_DIM, ROPE_THETA)
        ak = rotate(rms_norm(ak.reshape(b, s, A_HEADS, HEAD_DIM), a_k_norm[l]), pos, ROPE_DIM, ROPE_THETA)
        av = av.reshape(b, s, A_HEADS, HEAD_DIM)
        o_a = dilated_attention(aq, ak, av).reshape(b, s, A_OUT)

        bq = rotate(rms_norm(bq.reshape(b, s, B_Q_HEADS, HEAD_DIM), b_q_norm[l]), pos, ROPE_DIM, ROPE_THETA)
        bk = rotate(rms_norm(bk.reshape(b, s, B_KV_HEADS, HEAD_DIM), b_k_norm[l]), pos, ROPE_DIM, ROPE_THETA)
        bv = bv.reshape(b, s, B_KV_HEADS, HEAD_DIM)
        o_b, _ = banded_attention(bq.reshape(b, s, B_KV_HEADS, B_REP, HEAD_DIM), bk, bv,
                                  B_WINDOW - 1, sinks=b_sinks[l])
        o_b = o_b.reshape(b, s, B_Q_WIDTH)

        y = retention(cq.reshape(b, s, C_HEADS, C_HEAD_DIM), ck.reshape(b, s, C_HEADS, C_HEAD_DIM),
                      cv.reshape(b, s, C_HEADS, C_HEAD_DIM), pos)
        y = y * lax.rsqrt(jnp.mean(y * y, axis=-1, keepdims=True) + EPS)
        y = y.reshape(b, s, C_WIDTH) * c_gn[l].astype(jnp.float32)
        o_c = (jax.nn.silu(cg.astype(jnp.float32)) * y).astype(x.dtype)

        merged = (jax.nn.sigmoid(ga) * jnp.einsum('bse,ed->bsd', o_a, w_br_a[l])
                  + jax.nn.sigmoid(gb) * jnp.einsum('bse,ed->bsd', o_b, w_br_b[l])
                  + jax.nn.sigmoid(gc) * jnp.einsum('bse,ed->bsd', o_c, w_br_c[l]))
        x = x + jnp.einsum('bsd,de->bse', merged, w_out[l])

        hdn = jnp.einsum('bsd,df->bsf', rms_norm(x, mlp_norm[l]), w_up[l])
        hdn = jnp.square(jax.nn.relu(hdn))
        x = x + jnp.einsum('bsf,fd->bsd', hdn, w_down[l])
    return x
```

```python
import functools
import math

import jax
import jax.numpy as jnp
from jax import lax
from jax.experimental import pallas as pl
from jax.experimental.pallas import tpu as pltpu

F32 = jnp.float32
BF16 = jnp.bfloat16

D_MODEL = 2048
EPS = 1e-6
NEG_INF = -1e30
BLOCK = 128
LANES = 128
HEAD_DIM = 64
ROPE_DIM = HEAD_DIM // 4
ROPE_THETA = 500000.0
A_DILATED = ((128, 1), (512, 4), (2048, 16))
A_GROUPS = len(A_DILATED)
A_HEADS_PER_GROUP = 6
A_GROUP_WIDTH = A_HEADS_PER_GROUP * HEAD_DIM
A_PAIRS = A_GROUP_WIDTH // LANES
A_SUPER = BLOCK * max(d for _, d in A_DILATED)
B_WINDOW = 128
B_Q_HEADS = 16
B_KV_HEADS = 4
B_REP = B_Q_HEADS // B_KV_HEADS
B_Q_WIDTH = B_Q_HEADS * HEAD_DIM
B_KV_WIDTH = B_KV_HEADS * HEAD_DIM
B_PAIRS = B_Q_WIDTH // LANES
C_HEADS = 8
C_HEAD_DIM = 128
C_WIDTH = C_HEADS * C_HEAD_DIM
C_CHUNK = 128
C_ROT_THETA = 10000.0
D_FF = 4 * D_MODEL
IN_SIZES = (A_GROUPS * A_GROUP_WIDTH,) * 3 + (B_Q_WIDTH, B_KV_WIDTH, B_KV_WIDTH) + (C_WIDTH,) * 4 + (D_MODEL,) * 3

VMEM_LIMIT = 56 * 1024 * 1024


def _params(*sem):
    return pltpu.CompilerParams(dimension_semantics=sem, vmem_limit_bytes=VMEM_LIMIT)


def _sigmoid(a):
    return 1.0 / (1.0 + jnp.exp(-a))


def _rmsnorm_kernel(x_ref, g_ref, o_ref):
    x = x_ref[...]
    ms = jnp.mean(x * x, axis=-1, keepdims=True)
    o_ref[...] = (x * lax.rsqrt(ms + EPS) * g_ref[...]).astype(o_ref.dtype)


def _rmsnorm(x, g, tm=512):
    s, d = x.shape
    return pl.pallas_call(
        _rmsnorm_kernel,
        grid=(s // tm,),
        in_specs=[pl.BlockSpec((tm, d), lambda i: (i, 0)),
                  pl.BlockSpec((1, d), lambda i: (0, 0))],
        out_specs=pl.BlockSpec((tm, d), lambda i: (i, 0)),
        out_shape=jax.ShapeDtypeStruct((s, d), BF16),
        compiler_params=_params("parallel"),
        name="rmsnorm",
    )(x, g.reshape(1, d))


def _proj_kernel(kinds, *refs):
    refs = list(refs)
    o_ref = refs.pop()
    u_ref, w_ref = refs[0], refs[1]
    rest = refs[2:]
    need_vec = any(k in ("qk", "rot", "act") for k in kinds)
    need_tab = any(k in ("qk", "rot") for k in kinds)
    vec_ref = rest.pop(0) if need_vec else None
    tabs = [rest.pop(0) for _ in range(3)] if need_tab else None
    bd_ref = rest.pop(0) if "qk" in kinds else None

    acc = jnp.dot(u_ref[...], w_ref[...], preferred_element_type=F32)
    for c, kind in enumerate(kinds):
        sl = slice(c * LANES, (c + 1) * LANES)
        a = acc[:, sl]
        if kind == "qk":
            ss = jnp.dot((a * a).astype(BF16), bd_ref[...], preferred_element_type=F32)
            y = a * lax.rsqrt(ss * (1.0 / HEAD_DIM) + EPS) * vec_ref[:, sl]
            y = (y * tabs[0][...]
                 + pltpu.roll(y, LANES - ROPE_DIM // 2, 1) * tabs[1][...]
                 + pltpu.roll(y, ROPE_DIM // 2, 1) * tabs[2][...])
        elif kind == "rot":
            y = (a * tabs[0][...] + pltpu.roll(a, LANES // 2, 1) * tabs[1][...]) * vec_ref[:, sl]
        elif kind == "sigmoid":
            y = _sigmoid(a)
        elif kind == "act":
            y = jnp.where(vec_ref[:, sl] > 0.5, a * _sigmoid(a), a)
        else:
            y = a
        o_ref[:, sl] = y.astype(o_ref.dtype)


def _proj(u, w, kinds, tm, tn, vec=None, tabs=None, bd=None):
    s, k = u.shape
    n = w.shape[1]
    assert n % tn == 0 and s % tm == 0 and len(kinds) == tn // LANES
    args = [u, w]
    in_specs = [pl.BlockSpec((tm, k), lambda i, j: (i, 0)),
                pl.BlockSpec((k, tn), lambda i, j: (0, j))]
    if vec is not None:
        args.append(vec.reshape(1, n).astype(F32))
        in_specs.append(pl.BlockSpec((1, tn), lambda i, j: (0, j)))
    if tabs is not None:
        for t in tabs:
            args.append(t)
            in_specs.append(pl.BlockSpec((tm, LANES), lambda i, j: (i, 0)))
    if bd is not None:
        args.append(bd)
        in_specs.append(pl.BlockSpec((LANES, LANES), lambda i, j: (0, 0)))
    return pl.pallas_call(
        functools.partial(_proj_kernel, tuple(kinds)),
        grid=(s // tm, n // tn),
        in_specs=in_specs,
        out_specs=pl.BlockSpec((tm, tn), lambda i, j: (i, j)),
        out_shape=jax.ShapeDtypeStruct((s, n), BF16),
        compiler_params=_params("parallel", "arbitrary"),
        name="proj_" + kinds[0],
    )(*args)


def _half_masks(rows):
    lane = lax.broadcasted_iota(jnp.int32, (rows, LANES), 1)
    lo = jnp.where(lane < HEAD_DIM, 1.0, 0.0)
    return lo.astype(BF16), (1.0 - lo).astype(BF16)


def _pair_scores(q, k2):
    dn = (((1,), (1,)), ((), ()))
    lo, hi = _half_masks(BLOCK)
    se = lax.dot_general(q * lo, k2, dn, preferred_element_type=F32)
    so = lax.dot_general(q * hi, k2, dn, preferred_element_type=F32)
    return se, so


def _pair_attend(q, k2, v2, valid, sink=None):
    lane = lax.broadcasted_iota(jnp.int32, (BLOCK, LANES), 1)
    lo = lane < HEAD_DIM
    se, so = _pair_scores(q, k2)
    se = jnp.where(valid, se, NEG_INF)
    so = jnp.where(valid, so, NEG_INF)
    me = jnp.max(se, axis=-1, keepdims=True)
    mo = jnp.max(so, axis=-1, keepdims=True)
    mx = jnp.where(lo, me, mo)
    if sink is not None:
        mx = jnp.maximum(mx, sink)
        me = mx[:, :1]
        mo = mx[:, HEAD_DIM:HEAD_DIM + 1]
    p = jnp.concatenate([jnp.exp(se - me), jnp.exp(so - mo)], axis=1).astype(BF16)
    lo2, hi2 = _half_masks(2 * BLOCK)
    vx = jnp.concatenate(
        [jnp.concatenate([v2 * lo2, lo2], axis=1),
         jnp.concatenate([v2 * hi2, hi2], axis=1)], axis=0)
    r = jnp.dot(p, vx, preferred_element_type=F32)
    num, den = r[:, :LANES], r[:, LANES:]
    if sink is not None:
        den = den + jnp.exp(sink - mx)
    return num, den, mx


def _band_mask(max_dist):
    qi = lax.broadcasted_iota(jnp.int32, (BLOCK, 2 * BLOCK), 0)
    kj = lax.broadcasted_iota(jnp.int32, (BLOCK, 2 * BLOCK), 1)
    dist = qi + BLOCK - kj
    return (dist >= 0) & (dist <= max_dist), kj >= BLOCK


def _fill_ext(ext_ref, prev_ref, cur_ref):
    p = prev_ref.shape[0]
    ext_ref[0:p, :] = prev_ref[...]
    ext_ref[p:, :] = cur_ref[...]


def _attn_a_kernel(*refs):
    ins, (o_ref,), scr = refs[:15], refs[15:16], refs[16:]
    sb = pl.program_id(0)
    band, in_cur = _band_mask(BLOCK)
    o_scr, l_scr = scr[0], scr[1]
    for g, (_, dil) in enumerate(A_DILATED):
        q_ref, k_ref, v_ref, kp_ref, vp_ref = ins[5 * g:5 * g + 5]
        kx_ref, vx_ref = scr[2 + 2 * g], scr[3 + 2 * g]
        _fill_ext(kx_ref, kp_ref, k_ref)
        _fill_ext(vx_ref, vp_ref, v_ref)
        prev_rows = BLOCK * dil

        def body(b, carry, g=g, dil=dil, q_ref=q_ref, kx_ref=kx_ref, vx_ref=vx_ref, prev_rows=prev_rows):
            row = pl.multiple_of(b * BLOCK, BLOCK)
            q = q_ref[pl.ds(row, BLOCK), :]
            k2 = jnp.concatenate([kx_ref[pl.ds(row, BLOCK), :],
                                  kx_ref[pl.ds(row + prev_rows, BLOCK), :]], axis=0)
            v2 = jnp.concatenate([vx_ref[pl.ds(row, BLOCK), :],
                                  vx_ref[pl.ds(row + prev_rows, BLOCK), :]], axis=0)
            no_prev = jnp.logical_and(sb == 0, b < dil)
            valid = band & (in_cur | jnp.logical_not(no_prev))
            num, den, mx = _pair_attend(q, k2, v2, valid)
            o = num / den
            lse = mx + jnp.log(den)
            if dil == 1:
                dst = pl.ds(row, BLOCK)
            else:
                chunk = b // dil
                res = b - chunk * dil
                dst = pl.ds(chunk * prev_rows + res, BLOCK, stride=dil)
            o_scr[g, dst, :] = o
            l_scr[g, dst, :] = lse
            return carry

        lax.fori_loop(0, A_SUPER // BLOCK, body, 0)

    l0, l1, l2 = l_scr[0], l_scr[1], l_scr[2]
    lm = jnp.maximum(jnp.maximum(l0, l1), l2)
    w0, w1, w2 = jnp.exp(l0 - lm), jnp.exp(l1 - lm), jnp.exp(l2 - lm)
    o = (w0 * o_scr[0] + w1 * o_scr[1] + w2 * o_scr[2]) / (w0 + w1 + w2)
    o_ref[...] = o.astype(o_ref.dtype)


def _attn_a(qkv):
    s = qkv[0].shape[0]
    args, in_specs, scratch = [], [], [pltpu.VMEM((A_GROUPS, A_SUPER, LANES), F32),
                                       pltpu.VMEM((A_GROUPS, A_SUPER, LANES), F32)]
    for g, (_, dil) in enumerate(A_DILATED):
        prev_rows = BLOCK * dil
        per = A_SUPER // prev_rows

        def prev_map(off, per=per):
            return lambda i, p: (jnp.maximum(i * per - 1, 0), off + p)

        args += [qkv[g]] * 5
        in_specs += [pl.BlockSpec((A_SUPER, LANES), lambda i, p: (i, p)),
                     pl.BlockSpec((A_SUPER, LANES), lambda i, p: (i, A_PAIRS + p)),
                     pl.BlockSpec((A_SUPER, LANES), lambda i, p: (i, 2 * A_PAIRS + p)),
                     pl.BlockSpec((prev_rows, LANES), prev_map(A_PAIRS)),
                     pl.BlockSpec((prev_rows, LANES), prev_map(2 * A_PAIRS))]
        scratch += [pltpu.VMEM((prev_rows + A_SUPER, LANES), BF16)] * 2
    return pl.pallas_call(
        _attn_a_kernel,
        grid=(s // A_SUPER, A_PAIRS),
        in_specs=in_specs,
        out_specs=pl.BlockSpec((A_SUPER, LANES), lambda i, p: (i, p)),
        out_shape=jax.ShapeDtypeStruct((s, A_GROUP_WIDTH), BF16),
        scratch_shapes=scratch,
        compiler_params=_params("parallel", "parallel"),
        name="attn_a",
    )(*args)


def _attn_b_kernel(q_ref, k_ref, v_ref, kp_ref, vp_ref, sink_ref, o_ref, kx_ref, vx_ref):
    i = pl.program_id(0)
    p = pl.program_id(1)
    band, in_cur = _band_mask(B_WINDOW - 1)
    _fill_ext(kx_ref, kp_ref, k_ref)
    _fill_ext(vx_ref, vp_ref, v_ref)
    sink = sink_ref[pl.ds(p, 1), :]

    def body(b, carry):
        row = pl.multiple_of(b * BLOCK, BLOCK)
        q = q_ref[pl.ds(row, BLOCK), :]
        k2 = kx_ref[pl.ds(row, 2 * BLOCK), :]
        v2 = vx_ref[pl.ds(row, 2 * BLOCK), :]
        no_prev = jnp.logical_and(i == 0, b == 0)
        valid = band & (in_cur | jnp.logical_not(no_prev))
        num, den, _ = _pair_attend(q, k2, v2, valid, sink=sink)
        o_ref[pl.ds(row, BLOCK), :] = (num / den).astype(o_ref.dtype)
        return carry

    lax.fori_loop(0, q_ref.shape[0] // BLOCK, body, 0)


def _attn_b(qk, vmisc, sinks, tq=2048):
    s = qk.shape[0]
    per = tq // BLOCK
    kv = lambda p: p // (B_PAIRS // 2)
    return pl.pallas_call(
        _attn_b_kernel,
        grid=(s // tq, B_PAIRS),
        in_specs=[pl.BlockSpec((tq, LANES), lambda i, p: (i, p)),
                  pl.BlockSpec((tq, LANES), lambda i, p: (i, B_PAIRS + kv(p))),
                  pl.BlockSpec((tq, LANES), lambda i, p: (i, kv(p))),
                  pl.BlockSpec((BLOCK, LANES), lambda i, p: (jnp.maximum(i * per - 1, 0), B_PAIRS + kv(p))),
                  pl.BlockSpec((BLOCK, LANES), lambda i, p: (jnp.maximum(i * per - 1, 0), kv(p))),
                  pl.BlockSpec((B_PAIRS, LANES), lambda i, p: (0, 0))],
        out_specs=pl.BlockSpec((tq, LANES), lambda i, p: (i, p)),
        out_shape=jax.ShapeDtypeStruct((s, B_Q_WIDTH), BF16),
        scratch_shapes=[pltpu.VMEM((BLOCK + tq, LANES), BF16)] * 2,
        compiler_params=_params("parallel", "parallel"),
        name="attn_b",
    )(qk, qk, vmisc, qk, vmisc, sinks)


def _retention_kernel(q_ref, k_ref, v_ref, g_ref, gn_ref, tab_ref, o_ref, state_ref):
    i = pl.program_id(0)
    h = pl.program_id(1)

    @pl.when(i == 0)
    def _():
        state_ref[h] = jnp.zeros((C_HEAD_DIM, C_HEAD_DIM), F32)

    decay, q_decay, k_decay, chunk_decay = tab_ref[0, 0], tab_ref[0, 1], tab_ref[0, 2], tab_ref[0, 3]
    gn = gn_ref[...]

    def body(n, carry):
        row = pl.multiple_of(n * C_CHUNK, C_CHUNK)
        q = q_ref[pl.ds(row, C_CHUNK), :]
        k = k_ref[pl.ds(row, C_CHUNK), :]
        v = v_ref[pl.ds(row, C_CHUNK), :]
        state = state_ref[h]
        s = lax.dot_general(q, k, (((1,), (1,)), ((), ())), preferred_element_type=F32) * decay
        inner = jnp.dot(s.astype(BF16), v, preferred_element_type=F32)
        cross = jnp.dot(q, state.astype(BF16), preferred_element_type=F32) * q_decay
        kd_t = (k.astype(F32) * k_decay).T.astype(BF16)
        state_ref[h] = chunk_decay * state + jnp.dot(kd_t, v, preferred_element_type=F32)
        y = inner + cross
        y = y * lax.rsqrt(jnp.mean(y * y, axis=-1, keepdims=True) + EPS)
        o_ref[pl.ds(row, C_CHUNK), :] = (g_ref[pl.ds(row, C_CHUNK), :].astype(F32) * (y * gn)).astype(o_ref.dtype)
        return carry

    lax.fori_loop(0, q_ref.shape[0] // C_CHUNK, body, 0)


def _retention(cqk, vmisc, c_gn, tabs, tq=2048):
    s = cqk.shape[0]
    v_off = B_KV_WIDTH // LANES
    g_off = v_off + C_HEADS
    return pl.pallas_call(
        _retention_kernel,
        grid=(s // tq, C_HEADS),
        in_specs=[pl.BlockSpec((tq, LANES), lambda i, h: (i, h)),
                  pl.BlockSpec((tq, LANES), lambda i, h: (i, C_HEADS + h)),
                  pl.BlockSpec((tq, LANES), lambda i, h: (i, v_off + h)),
                  pl.BlockSpec((tq, LANES), lambda i, h: (i, g_off + h)),
                  pl.BlockSpec((1, LANES), lambda i, h: (0, h)),
                  pl.BlockSpec((1, 4, C_CHUNK, C_CHUNK), lambda i, h: (h, 0, 0, 0))],
        out_specs=pl.BlockSpec((tq, LANES), lambda i, h: (i, h)),
        out_shape=jax.ShapeDtypeStruct((s, C_WIDTH), BF16),
        scratch_shapes=[pltpu.VMEM((C_HEADS, C_HEAD_DIM, C_HEAD_DIM), F32)],
        compiler_params=_params("arbitrary", "arbitrary"),
        name="retention",
    )(cqk, cqk, vmisc, vmisc, c_gn.reshape(1, C_WIDTH).astype(F32), tabs)


def _branch_kernel(oa_ref, ob_ref, oc_ref, wa_ref, wb_ref, wc_ref, ga_ref, gb_ref, gc_ref, o_ref):
    a = jnp.dot(oa_ref[...], wa_ref[...], preferred_element_type=F32)
    m = ga_ref[...].astype(F32) * a
    b = jnp.dot(ob_ref[...], wb_ref[...], preferred_element_type=F32)
    m = m + gb_ref[...].astype(F32) * b
    c = jnp.dot(oc_ref[...], wc_ref[...], preferred_element_type=F32)
    m = m + gc_ref[...].astype(F32) * c
    o_ref[...] = m.astype(o_ref.dtype)


def _branch(o_a, o_b, o_c, w_a, w_b, w_c, gates, tm=1024, tn=1024):
    s = o_a.shape[0]
    n = w_a.shape[1]
    nt = n // tn
    act = lambda width: pl.BlockSpec((tm, width), lambda i, j: (i, 0))
    wgt = lambda width: pl.BlockSpec((width, tn), lambda i, j: (0, j))
    gate = lambda off: pl.BlockSpec((tm, tn), lambda i, j: (i, off * nt + j))
    return pl.pallas_call(
        _branch_kernel,
        grid=(s // tm, nt),
        in_specs=[act(o_a.shape[1]), act(o_b.shape[1]), act(o_c.shape[1]),
                  wgt(w_a.shape[0]), wgt(w_b.shape[0]), wgt(w_c.shape[0]),
                  gate(0), gate(1), gate(2)],
        out_specs=pl.BlockSpec((tm, tn), lambda i, j: (i, j)),
        out_shape=jax.ShapeDtypeStruct((s, n), BF16),
        compiler_params=_params("parallel", "arbitrary"),
        name="branch",
    )(o_a, o_b, o_c, w_a, w_b, w_c, gates, gates, gates)


def _outproj_kernel(m_ref, w_ref, x_ref, g_ref, x1_ref, u_ref):
    x1 = x_ref[...] + jnp.dot(m_ref[...], w_ref[...], preferred_element_type=F32)
    x1_ref[...] = x1
    ms = jnp.mean(x1 * x1, axis=-1, keepdims=True)
    u_ref[...] = (x1 * lax.rsqrt(ms + EPS) * g_ref[...]).astype(u_ref.dtype)


def _outproj(merged, w_out, x, g, tm=512):
    s, d = x.shape
    return pl.pallas_call(
        _outproj_kernel,
        grid=(s // tm,),
        in_specs=[pl.BlockSpec((tm, d), lambda i: (i, 0)),
                  pl.BlockSpec((d, d), lambda i: (0, 0), pipeline_mode=pl.Buffered(1)),
                  pl.BlockSpec((tm, d), lambda i: (i, 0)),
                  pl.BlockSpec((1, d), lambda i: (0, 0))],
        out_specs=[pl.BlockSpec((tm, d), lambda i: (i, 0)),
                   pl.BlockSpec((tm, d), lambda i: (i, 0))],
        out_shape=[jax.ShapeDtypeStruct((s, d), F32), jax.ShapeDtypeStruct((s, d), BF16)],
        compiler_params=_params("parallel"),
        name="outproj",
    )(merged, w_out, x, g.reshape(1, d))


def _mlp_kernel(u_ref, wu_ref, wd_ref, x_ref, o_ref):
    f = pl.program_id(1)

    @pl.when(f == 0)
    def _():
        o_ref[...] = x_ref[...]

    h = jnp.dot(u_ref[...], wu_ref[...], preferred_element_type=F32)
    h = jnp.square(jnp.maximum(h, 0.0)).astype(BF16)
    o_ref[...] += jnp.dot(h, wd_ref[...], preferred_element_type=F32)


def _mlp(u, w_up, w_down, x, tm=512, tf=1024):
    s, d = x.shape
    f = w_up.shape[1]
    return pl.pallas_call(
        _mlp_kernel,
        grid=(s // tm, f // tf),
        in_specs=[pl.BlockSpec((tm, d), lambda i, j: (i, 0)),
                  pl.BlockSpec((d, tf), lambda i, j: (0, j)),
                  pl.BlockSpec((tf, d), lambda i, j: (j, 0)),
                  pl.BlockSpec((tm, d), lambda i, j: (i, 0))],
        out_specs=pl.BlockSpec((tm, d), lambda i, j: (i, 0)),
        out_shape=jax.ShapeDtypeStruct((s, d), F32),
        compiler_params=_params("parallel", "arbitrary"),
        name="mlp",
    )(u, w_up, w_down, x)


def _residue_major(t, dil):
    if dil == 1:
        return t
    s = t.shape[0]
    rest = t.shape[1:]
    return t.reshape((s // (BLOCK * dil), BLOCK, dil) + rest).swapaxes(1, 2).reshape((s,) + rest)


def _rope_tables(pos):
    half = ROPE_DIM // 2
    inv = ROPE_THETA ** (-jnp.arange(half, dtype=F32) / half)
    ang = pos.astype(F32)[:, None] * inv[None, :]
    cos, sin = jnp.cos(ang), jnp.sin(ang)
    n = pos.shape[0]
    pad = HEAD_DIM - ROPE_DIM
    t0 = jnp.concatenate([cos, cos, jnp.ones((n, pad), F32)], axis=1)
    t1 = jnp.concatenate([-sin, jnp.zeros((n, half + pad), F32)], axis=1)
    t2 = jnp.concatenate([jnp.zeros((n, half), F32), sin, jnp.zeros((n, pad), F32)], axis=1)
    return tuple(jnp.tile(t, (1, LANES // HEAD_DIM)) for t in (t0, t1, t2))


def _retention_rot_tables(pos):
    half = C_HEAD_DIM // 2
    inv = C_ROT_THETA ** (-jnp.arange(half, dtype=F32) / half)
    ang = pos.astype(F32)[:, None] * inv[None, :]
    cos, sin = jnp.cos(ang), jnp.sin(ang)
    t0 = jnp.concatenate([cos, cos], axis=1)
    t1 = jnp.concatenate([-sin, sin], axis=1)
    return t0, t1, t1


def _retention_decay_tables():
    c = C_CHUNK
    log_g = jnp.log1p(-(2.0 ** (-5.0 - jnp.arange(C_HEADS, dtype=F32))))
    i = jnp.arange(c, dtype=F32)
    rel = i[:, None] - i[None, :]
    decay = jnp.where(rel >= 0, jnp.exp(log_g[:, None, None] * jnp.maximum(rel, 0.0)), 0.0)
    q_decay = jnp.broadcast_to(jnp.exp(log_g[:, None] * (i + 1.0)[None, :])[:, :, None], (C_HEADS, c, c))
    k_decay = jnp.broadcast_to(jnp.exp(log_g[:, None] * (c - 1 - i)[None, :])[:, :, None], (C_HEADS, c, c))
    chunk_decay = jnp.broadcast_to(jnp.exp(log_g * c)[:, None, None], (C_HEADS, c, c))
    return jnp.stack([decay, q_decay, k_decay, chunk_decay], axis=1)


def _b_head_order():
    order = []
    for p in range(B_PAIRS):
        g2, a = divmod(p, B_PAIRS // 2)
        order += [B_REP * (2 * g2) + a, B_REP * (2 * g2 + 1) + a]
    return order


def _head_cols(order):
    return jnp.concatenate([jnp.arange(h * HEAD_DIM, (h + 1) * HEAD_DIM) for h in order])


def _split_in(w):
    outs, start = [], 0
    for size in IN_SIZES:
        outs.append(w[:, start:start + size])
        start += size
    return outs


def kernel(x, mix_norm, w_in, a_q_norm, a_k_norm, b_q_norm, b_k_norm, b_sinks, c_gn,
           w_br_a, w_br_b, w_br_c, w_out, mlp_norm, w_up, w_down):
    b, s, d = x.shape
    assert b == 1 and d == D_MODEL and s % A_SUPER == 0
    depth = w_in.shape[0]
    x = x.reshape(s, d)
    pos = jnp.arange(s)
    rope = [_rope_tables(_residue_major(pos, dil)) for _, dil in A_DILATED]
    rot_c = _retention_rot_tables(pos)
    decay_tabs = _retention_decay_tables()
    bd = (jnp.arange(LANES)[:, None] // HEAD_DIM == jnp.arange(LANES)[None, :] // HEAD_DIM).astype(BF16)
    b_order = _b_head_order()
    b_cols = _head_cols(b_order)
    scale = HEAD_DIM ** -0.5
    tile_gain = lambda g, heads: jnp.tile(g, heads)

    u = _rmsnorm(x, mix_norm[0])
    for l in range(depth):
        aq, ak, av, bq, bk, bv, cq, ck, cv, cg, ga, gb, gc = _split_in(w_in[l])

        qkv = []
        for g, (_, dil) in enumerate(A_DILATED):
            sl = slice(g * A_GROUP_WIDTH, (g + 1) * A_GROUP_WIDTH)
            w_g = jnp.concatenate([aq[:, sl], ak[:, sl], av[:, sl]], axis=1).astype(BF16)
            vec = jnp.concatenate([tile_gain(a_q_norm[l] * scale, A_HEADS_PER_GROUP),
                                   tile_gain(a_k_norm[l], A_HEADS_PER_GROUP),
                                   jnp.ones((A_GROUP_WIDTH,), F32)])
            kinds = ["qk"] * (2 * A_PAIRS) + ["plain"] * A_PAIRS
            qkv.append(_proj(_residue_major(u, dil), w_g, kinds, tm=1024, tn=3 * A_GROUP_WIDTH,
                             vec=vec, tabs=rope[g], bd=bd))
        o_a = _attn_a(qkv)

        w_bqk = jnp.concatenate([bq[:, b_cols], bk], axis=1).astype(BF16)
        vec = jnp.concatenate([tile_gain(b_q_norm[l] * scale, B_Q_HEADS), tile_gain(b_k_norm[l], B_KV_HEADS)])
        bqk = _proj(u, w_bqk, ["qk"] * (B_PAIRS + B_KV_HEADS // 2), tm=1024, tn=B_Q_WIDTH + B_KV_WIDTH,
                    vec=vec, tabs=rope[0], bd=bd)
        w_vm = jnp.concatenate([bv, cv, cg], axis=1).astype(BF16)
        mode = jnp.concatenate([jnp.zeros((B_KV_WIDTH + C_WIDTH,), F32), jnp.ones((C_WIDTH,), F32)])
        vmisc = _proj(u, w_vm, ["act"] * 9, tm=1024, tn=1152, vec=mode)
        sink_pairs = jnp.repeat(b_sinks[l][jnp.array(b_order)].astype(F32), HEAD_DIM).reshape(B_PAIRS, LANES)
        o_b = _attn_b(bqk, vmisc, sink_pairs)

        w_cqk = jnp.concatenate([cq, ck], axis=1).astype(BF16)
        vec = jnp.concatenate([jnp.ones((C_WIDTH,), F32), jnp.full((C_WIDTH,), C_HEAD_DIM ** -0.5, F32)])
        cqk = _proj(u, w_cqk, ["rot"] * 8, tm=1024, tn=1024, vec=vec, tabs=rot_c)
        o_c = _retention(cqk, vmisc, c_gn[l], decay_tabs)

        w_g = jnp.concatenate([ga, gb, gc], axis=1).astype(BF16)
        gates = _proj(u, w_g, ["sigmoid"] * 8, tm=1024, tn=1024)
        w_b_perm = w_br_b[l][b_cols, :]
        merged = _branch(o_a, o_b, o_c, w_br_a[l].astype(BF16), w_b_perm.astype(BF16),
                         w_br_c[l].astype(BF16), gates)
        x, u2 = _outproj(merged, w_out[l].astype(BF16), x, mlp_norm[l])
        x = _mlp(u2, w_up[l].astype(BF16), w_down[l].astype(BF16), x)
        if l + 1 < depth:
            u = _rmsnorm(x, mix_norm[l + 1])
    return x.reshape(b, s, d)
```

```python
import functools
import math

import jax
import jax.numpy as jnp
from jax import lax
from jax.experimental import pallas as pl
from jax.experimental.pallas import tpu as pltpu

F32 = jnp.float32
BF16 = jnp.bfloat16

D_MODEL = 2048
EPS = 1e-6
NEG_INF = -1e30
BLOCK = 128
LANES = 128
HEAD_DIM = 64
ROPE_DIM = HEAD_DIM // 4
ROPE_THETA = 500000.0
A_DILATED = ((128, 1), (512, 4), (2048, 16))
A_GROUPS = len(A_DILATED)
A_HEADS_PER_GROUP = 6
A_GROUP_WIDTH = A_HEADS_PER_GROUP * HEAD_DIM
A_PAIRS = A_GROUP_WIDTH // LANES
A_SUPER = BLOCK * max(d for _, d in A_DILATED)
B_WINDOW = 128
B_Q_HEADS = 16
B_KV_HEADS = 4
B_REP = B_Q_HEADS // B_KV_HEADS
B_Q_WIDTH = B_Q_HEADS * HEAD_DIM
B_KV_WIDTH = B_KV_HEADS * HEAD_DIM
B_PAIRS = B_Q_WIDTH // LANES
C_HEADS = 8
C_HEAD_DIM = 128
C_WIDTH = C_HEADS * C_HEAD_DIM
C_ROT_THETA = 10000.0
RET_CHUNK = 256
RET_UNROLL = 4
D_FF = 4 * D_MODEL
IN_SIZES = (A_GROUPS * A_GROUP_WIDTH,) * 3 + (B_Q_WIDTH, B_KV_WIDTH, B_KV_WIDTH) + (C_WIDTH,) * 4 + (D_MODEL,) * 3

VMEM_LIMIT = 56 * 1024 * 1024
LOOP_UNROLL = 8
LOG2E = math.log2(math.e)


def _params(*sem):
    return pltpu.CompilerParams(dimension_semantics=sem, vmem_limit_bytes=VMEM_LIMIT)


def _sigmoid(a):
    return 1.0 / (1.0 + jnp.exp(-a))


NORM_ROWS = 512


def _store_layouts(y, y_scr, o1_ref, o4_ref, o16_ref):
    o1_ref[...] = y.astype(BF16)
    slabs = y.shape[1] // LANES
    for c in range(slabs):
        y_scr[c] = y[:, c * LANES:(c + 1) * LANES]
    d4, d16 = A_DILATED[1][1], A_DILATED[2][1]
    for c in range(slabs):
        sl = slice(c * LANES, (c + 1) * LANES)
        for r in range(d4):
            o4_ref[r * BLOCK:(r + 1) * BLOCK, sl] = y_scr[c, pl.ds(r, BLOCK, stride=d4), :].astype(BF16)
        for r in range(d16):
            o16_ref[0, r, :, sl] = y_scr[c, pl.ds(r, NORM_ROWS // d16, stride=d16), :].astype(BF16)


def _layout_out_specs(row_map):
    d = D_MODEL
    d16 = A_DILATED[2][1]
    per = A_SUPER // NORM_ROWS
    return [pl.BlockSpec((NORM_ROWS, d), lambda *a: (row_map(*a), 0)),
            pl.BlockSpec((NORM_ROWS, d), lambda *a: (row_map(*a), 0)),
            pl.BlockSpec((1, d16, NORM_ROWS // d16, d), lambda *a: (row_map(*a) // per, 0, row_map(*a) % per, 0))]


def _layout_out_shapes(s):
    d16 = A_DILATED[2][1]
    return [jax.ShapeDtypeStruct((s, D_MODEL), BF16),
            jax.ShapeDtypeStruct((s, D_MODEL), BF16),
            jax.ShapeDtypeStruct((s // A_SUPER, d16, BLOCK, D_MODEL), BF16)]


def _rmsnorm_kernel(x_ref, g_ref, o1_ref, o4_ref, o16_ref, y_scr):
    x = x_ref[...]
    ms = jnp.mean(x * x, axis=-1, keepdims=True)
    _store_layouts(x * lax.rsqrt(ms + EPS) * g_ref[...], y_scr, o1_ref, o4_ref, o16_ref)


def _rmsnorm(x, g):
    s, d = x.shape
    u1, u4, u16 = pl.pallas_call(
        _rmsnorm_kernel,
        grid=(s // NORM_ROWS,),
        in_specs=[pl.BlockSpec((NORM_ROWS, d), lambda i: (i, 0)),
                  pl.BlockSpec((1, d), lambda i: (0, 0))],
        out_specs=_layout_out_specs(lambda i: i),
        out_shape=_layout_out_shapes(s),
        scratch_shapes=[pltpu.VMEM((d // LANES, NORM_ROWS, LANES), F32)],
        compiler_params=_params("parallel"),
        name="rmsnorm",
    )(x, g.reshape(1, d))
    return u1, u4, u16.reshape(s, d)


def _proj_kernel(kinds, *refs):
    refs = list(refs)
    o_ref = refs.pop()
    u_ref, w_ref = refs[0], refs[1]
    rest = refs[2:]
    need_vec = any(k in ("qk", "rot", "act") for k in kinds)
    need_tab = any(k in ("qk", "rot") for k in kinds)
    vec_ref = rest.pop(0) if need_vec else None
    tabs = [rest.pop(0) for _ in range(3)] if need_tab else None
    bd_ref = rest.pop(0) if "qk" in kinds else None

    acc = jnp.dot(u_ref[...], w_ref[...], preferred_element_type=F32)
    for c, kind in enumerate(kinds):
        sl = slice(c * LANES, (c + 1) * LANES)
        a = acc[:, sl]
        if kind == "qk":
            ss = jnp.dot((a * a).astype(BF16), bd_ref[...], preferred_element_type=F32)
            y = a * lax.rsqrt(ss * (1.0 / HEAD_DIM) + EPS) * vec_ref[:, sl]
            y = (y * tabs[0][...]
                 + pltpu.roll(y, LANES - ROPE_DIM // 2, 1) * tabs[1][...]
                 + pltpu.roll(y, ROPE_DIM // 2, 1) * tabs[2][...])
        elif kind == "rot":
            y = (a * tabs[0][...] + pltpu.roll(a, LANES // 2, 1) * tabs[1][...]) * vec_ref[:, sl]
        elif kind == "sigmoid":
            y = _sigmoid(a)
        elif kind == "act":
            y = jnp.where(vec_ref[:, sl] > 0.5, a * _sigmoid(a), a)
        else:
            y = a
        o_ref[:, sl] = y.astype(o_ref.dtype)


def _proj(u, w, layer, col0, n, kinds, tm, tn, vec=None, tabs=None, bd=None):
    s, k = u.shape
    assert n % tn == 0 and col0 % tn == 0 and s % tm == 0 and len(kinds) == tn // LANES
    j0 = col0 // tn
    args = [u, w]
    in_specs = [pl.BlockSpec((tm, k), lambda i, j: (i, 0)),
                pl.BlockSpec((None, k, tn), lambda i, j: (layer, 0, j0 + j))]
    if vec is not None:
        args.append(vec.reshape(1, n).astype(F32))
        in_specs.append(pl.BlockSpec((1, tn), lambda i, j: (0, j)))
    if tabs is not None:
        for t in tabs:
            args.append(t)
            in_specs.append(pl.BlockSpec((tm, LANES), lambda i, j: (i, 0)))
    if bd is not None:
        args.append(bd)
        in_specs.append(pl.BlockSpec((LANES, LANES), lambda i, j: (0, 0)))
    return pl.pallas_call(
        functools.partial(_proj_kernel, tuple(kinds)),
        grid=(s // tm, n // tn),
        in_specs=in_specs,
        out_specs=pl.BlockSpec((tm, tn), lambda i, j: (i, j)),
        out_shape=jax.ShapeDtypeStruct((s, n), BF16),
        compiler_params=_params("parallel", "arbitrary"),
        name="proj_" + kinds[0],
    )(*args)


def _half_masks(rows):
    lane = lax.broadcasted_iota(jnp.int32, (rows, LANES), 1)
    lo = jnp.where(lane < HEAD_DIM, 1.0, 0.0)
    return lo.astype(BF16), (1.0 - lo).astype(BF16)


def _pair_attend(q, k2, v2, bias, sink_key=False):
    if sink_key:
        rows = 16
        keep = jnp.where(lax.broadcasted_iota(jnp.int32, (rows, LANES), 0) > 0, 1.0, 0.0).astype(BF16)
        k2 = jnp.concatenate([k2[:rows] * keep, k2[rows:]], axis=0)
        v2 = jnp.concatenate([v2[:rows] * keep, v2[rows:]], axis=0)
    lo_b, hi_b = _half_masks(BLOCK)
    q2 = jnp.concatenate([q * lo_b, q * hi_b], axis=0)
    s2 = lax.dot_general(q2, k2, (((1,), (1,)), ((), ())), preferred_element_type=F32) + bias
    m2 = jnp.max(s2, axis=-1, keepdims=True)
    p2 = jnp.exp2(s2 - m2).astype(BF16)
    vx = jnp.concatenate([v2, jnp.ones_like(v2)], axis=1)
    r2 = jnp.dot(p2, vx, preferred_element_type=F32)
    lo = lax.broadcasted_iota(jnp.int32, (BLOCK, LANES), 1) < HEAD_DIM
    num = jnp.where(lo, r2[:BLOCK, :LANES], r2[BLOCK:, :LANES])
    den = jnp.where(lo, r2[:BLOCK, LANES:], r2[BLOCK:, LANES:])
    mx = jnp.where(lo, m2[:BLOCK], m2[BLOCK:])
    return num, den, mx


def _fill_bias(bias_ref, max_dist, sinks=None):
    row = lax.broadcasted_iota(jnp.int32, (2 * BLOCK, 2 * BLOCK), 0)
    kj = lax.broadcasted_iota(jnp.int32, (2 * BLOCK, 2 * BLOCK), 1)
    dist = (row & (BLOCK - 1)) + BLOCK - kj
    band = jnp.where(dist >= 0, jnp.where(dist <= max_dist, 0.0, NEG_INF), NEG_INF)
    first = jnp.where(kj >= BLOCK, band, NEG_INF)
    if sinks is not None:
        assert max_dist < BLOCK
        sink = jnp.where(row < BLOCK, sinks[0], sinks[1])
        band = jnp.where(kj == 0, sink, band)
        first = jnp.where(kj == 0, sink, first)
    bias_ref[0] = band
    bias_ref[1] = first


def _fill_ext(ext_ref, prev_ref, cur_ref):
    p = prev_ref.shape[0]
    ext_ref[0:p, :] = prev_ref[...]
    ext_ref[p:, :] = cur_ref[...]


def _attn_a_kernel(*refs):
    ins, (o_ref,), scr = refs[:15], refs[15:16], refs[16:]
    sb = pl.program_id(0)
    o_scr, l_scr, bias_ref = scr[0], scr[1], scr[2]
    _fill_bias(bias_ref, BLOCK)
    for g, (_, dil) in enumerate(A_DILATED):
        q_ref, k_ref, v_ref, kp_ref, vp_ref = ins[5 * g:5 * g + 5]
        kx_ref, vx_ref = scr[3 + 2 * g], scr[4 + 2 * g]
        _fill_ext(kx_ref, kp_ref, k_ref)
        _fill_ext(vx_ref, vp_ref, v_ref)
        prev_rows = BLOCK * dil

        def body(b, carry, g=g, dil=dil, q_ref=q_ref, kx_ref=kx_ref, vx_ref=vx_ref, prev_rows=prev_rows):
            row = pl.multiple_of(b * BLOCK, BLOCK)
            q = q_ref[pl.ds(row, BLOCK), :]
            k2 = jnp.concatenate([kx_ref[pl.ds(row, BLOCK), :],
                                  kx_ref[pl.ds(row + prev_rows, BLOCK), :]], axis=0)
            v2 = jnp.concatenate([vx_ref[pl.ds(row, BLOCK), :],
                                  vx_ref[pl.ds(row + prev_rows, BLOCK), :]], axis=0)
            no_prev = jnp.logical_and(sb == 0, b < dil)
            num, den, mx = _pair_attend(q, k2, v2, bias_ref[no_prev.astype(jnp.int32)])
            if dil == 1:
                dst = pl.ds(row, BLOCK)
            else:
                chunk = b // dil
                res = b - chunk * dil
                dst = pl.ds(chunk * prev_rows + res, BLOCK, stride=dil)
            o_scr[g, dst, :] = num / den
            l_scr[g, dst, :] = mx + jnp.log2(den)
            return carry

        lax.fori_loop(0, A_SUPER // BLOCK, body, 0, unroll=LOOP_UNROLL)

    l0, l1, l2 = l_scr[0], l_scr[1], l_scr[2]
    lm = jnp.maximum(jnp.maximum(l0, l1), l2)
    w0, w1, w2 = jnp.exp2(l0 - lm), jnp.exp2(l1 - lm), jnp.exp2(l2 - lm)
    o = (w0 * o_scr[0] + w1 * o_scr[1] + w2 * o_scr[2]) / (w0 + w1 + w2)
    o_ref[...] = o.astype(o_ref.dtype)


def _attn_a(qkv):
    s = qkv[0].shape[0]
    args, in_specs = [], []
    scratch = [pltpu.VMEM((A_GROUPS, A_SUPER, LANES), F32),
               pltpu.VMEM((A_GROUPS, A_SUPER, LANES), F32),
               pltpu.VMEM((2, 2 * BLOCK, 2 * BLOCK), F32)]
    for g, (_, dil) in enumerate(A_DILATED):
        prev_rows = BLOCK * dil
        per = A_SUPER // prev_rows

        def prev_map(off, per=per):
            return lambda i, p: (jnp.maximum(i * per - 1, 0), off + p)

        args += [qkv[g]] * 5
        in_specs += [pl.BlockSpec((A_SUPER, LANES), lambda i, p: (i, p)),
                     pl.BlockSpec((A_SUPER, LANES), lambda i, p: (i, A_PAIRS + p)),
                     pl.BlockSpec((A_SUPER, LANES), lambda i, p: (i, 2 * A_PAIRS + p)),
                     pl.BlockSpec((prev_rows, LANES), prev_map(A_PAIRS)),
                     pl.BlockSpec((prev_rows, LANES), prev_map(2 * A_PAIRS))]
        scratch += [pltpu.VMEM((prev_rows + A_SUPER, LANES), BF16)] * 2
    return pl.pallas_call(
        _attn_a_kernel,
        grid=(s // A_SUPER, A_PAIRS),
        in_specs=in_specs,
        out_specs=pl.BlockSpec((A_SUPER, LANES), lambda i, p: (i, p)),
        out_shape=jax.ShapeDtypeStruct((s, A_GROUP_WIDTH), BF16),
        scratch_shapes=scratch,
        compiler_params=_params("parallel", "parallel"),
        name="attn_a",
    )(*args)


def _attn_b_kernel(q_ref, k_ref, v_ref, kp_ref, vp_ref, sink_ref, o_ref, kx_ref, vx_ref, bias_ref):
    i = pl.program_id(0)
    p = pl.program_id(1)
    _fill_bias(bias_ref, B_WINDOW - 1, sinks=(sink_ref[2 * p], sink_ref[2 * p + 1]))
    _fill_ext(kx_ref, kp_ref, k_ref)
    _fill_ext(vx_ref, vp_ref, v_ref)

    def body(b, carry):
        row = pl.multiple_of(b * BLOCK, BLOCK)
        q = q_ref[pl.ds(row, BLOCK), :]
        k2 = kx_ref[pl.ds(row, 2 * BLOCK), :]
        v2 = vx_ref[pl.ds(row, 2 * BLOCK), :]
        no_prev = jnp.logical_and(i == 0, b == 0)
        num, den, _ = _pair_attend(q, k2, v2, bias_ref[no_prev.astype(jnp.int32)], sink_key=True)
        o_ref[pl.ds(row, BLOCK), :] = (num / den).astype(o_ref.dtype)
        return carry

    lax.fori_loop(0, q_ref.shape[0] // BLOCK, body, 0, unroll=LOOP_UNROLL)


def _attn_b(qk, vmisc, sinks, tq=2048):
    s = qk.shape[0]
    per = tq // BLOCK
    kv = lambda p: p // (B_PAIRS // 2)
    return pl.pallas_call(
        _attn_b_kernel,
        grid=(s // tq, B_PAIRS),
        in_specs=[pl.BlockSpec((tq, LANES), lambda i, p: (i, p)),
                  pl.BlockSpec((tq, LANES), lambda i, p: (i, B_PAIRS + kv(p))),
                  pl.BlockSpec((tq, LANES), lambda i, p: (i, kv(p))),
                  pl.BlockSpec((BLOCK, LANES), lambda i, p: (jnp.maximum(i * per - 1, 0), B_PAIRS + kv(p))),
                  pl.BlockSpec((BLOCK, LANES), lambda i, p: (jnp.maximum(i * per - 1, 0), kv(p))),
                  pl.BlockSpec(memory_space=pltpu.SMEM)],
        out_specs=pl.BlockSpec((tq, LANES), lambda i, p: (i, p)),
        out_shape=jax.ShapeDtypeStruct((s, B_Q_WIDTH), BF16),
        scratch_shapes=[pltpu.VMEM((BLOCK + tq, LANES), BF16)] * 2
                       + [pltpu.VMEM((2, 2 * BLOCK, 2 * BLOCK), F32)],
        compiler_params=_params("parallel", "parallel"),
        name="attn_b",
    )(qk, qk, vmisc, qk, vmisc, sinks)


def _retention_kernel(q_ref, k_ref, v_ref, g_ref, gn_ref, decay_ref, qd_ref, kd_ref, cd_ref, o_ref, state_ref):
    i = pl.program_id(0)
    h = pl.program_id(1)

    @pl.when(i == 0)
    def _():
        state_ref[h] = jnp.zeros((C_HEAD_DIM, C_HEAD_DIM), F32)

    gn = gn_ref[...]

    def body(n, state):
        row = pl.multiple_of(n * RET_CHUNK, RET_CHUNK)
        q = q_ref[pl.ds(row, RET_CHUNK), :]
        k = k_ref[pl.ds(row, RET_CHUNK), :]
        v = v_ref[pl.ds(row, RET_CHUNK), :]
        s = lax.dot_general(q, k, (((1,), (1,)), ((), ())), preferred_element_type=F32) * decay_ref[0]
        inner = jnp.dot(s.astype(BF16), v, preferred_element_type=F32)
        cross = jnp.dot(q, state.astype(BF16), preferred_element_type=F32) * qd_ref[0]
        kd_t = (k.astype(F32) * kd_ref[0]).T.astype(BF16)
        new_state = cd_ref[0] * state + jnp.dot(kd_t, v, preferred_element_type=F32)
        y = inner + cross
        y = y * lax.rsqrt(jnp.mean(y * y, axis=-1, keepdims=True) + EPS)
        o_ref[pl.ds(row, RET_CHUNK), :] = (g_ref[pl.ds(row, RET_CHUNK), :].astype(F32) * (y * gn)).astype(o_ref.dtype)
        return new_state

    state_ref[h] = lax.fori_loop(0, q_ref.shape[0] // RET_CHUNK, body, state_ref[h], unroll=RET_UNROLL)


def _retention(cqk, vmisc, c_gn, tabs, tq=2048):
    s = cqk.shape[0]
    v_off = B_KV_WIDTH // LANES
    g_off = v_off + C_HEADS
    tab = lambda t: pl.BlockSpec((1,) + t.shape[1:], lambda i, h: (h, 0, 0))
    return pl.pallas_call(
        _retention_kernel,
        grid=(s // tq, C_HEADS),
        in_specs=[pl.BlockSpec((tq, LANES), lambda i, h: (i, h)),
                  pl.BlockSpec((tq, LANES), lambda i, h: (i, C_HEADS + h)),
                  pl.BlockSpec((tq, LANES), lambda i, h: (i, v_off + h)),
                  pl.BlockSpec((tq, LANES), lambda i, h: (i, g_off + h)),
                  pl.BlockSpec((1, LANES), lambda i, h: (0, h))] + [tab(t) for t in tabs],
        out_specs=pl.BlockSpec((tq, LANES), lambda i, h: (i, h)),
        out_shape=jax.ShapeDtypeStruct((s, C_WIDTH), BF16),
        scratch_shapes=[pltpu.VMEM((C_HEADS, C_HEAD_DIM, C_HEAD_DIM), F32)],
        compiler_params=_params("arbitrary", "arbitrary"),
        name="retention",
    )(cqk, cqk, vmisc, vmisc, c_gn.reshape(1, C_WIDTH).astype(F32), *tabs)


def _branch_kernel(oa_ref, ob_ref, oc_ref, wa_ref, wb_ref, wc_ref, ga_ref, gb_ref, gc_ref, o_ref):
    a = jnp.dot(oa_ref[...], wa_ref[...], preferred_element_type=F32)
    m = ga_ref[...].astype(F32) * a
    b = jnp.dot(ob_ref[...], wb_ref[...], preferred_element_type=F32)
    m = m + gb_ref[...].astype(F32) * b
    c = jnp.dot(oc_ref[...], wc_ref[...], preferred_element_type=F32)
    m = m + gc_ref[...].astype(F32) * c
    o_ref[...] = m.astype(o_ref.dtype)


def _branch(o_a, o_b, o_c, w_a, w_b, w_c, layer, gates, tm=1024, tn=1024):
    s = o_a.shape[0]
    n = w_a.shape[2]
    nt = n // tn
    act = lambda width: pl.BlockSpec((tm, width), lambda i, j: (i, 0))
    wgt = lambda width: pl.BlockSpec((None, width, tn), lambda i, j: (layer, 0, j))
    gate = lambda off: pl.BlockSpec((tm, tn), lambda i, j: (i, off * nt + j))
    return pl.pallas_call(
        _branch_kernel,
        grid=(s // tm, nt),
        in_specs=[act(o_a.shape[1]), act(o_b.shape[1]), act(o_c.shape[1]),
                  wgt(w_a.shape[1]), wgt(w_b.shape[1]), wgt(w_c.shape[1]),
                  gate(0), gate(1), gate(2)],
        out_specs=pl.BlockSpec((tm, tn), lambda i, j: (i, j)),
        out_shape=jax.ShapeDtypeStruct((s, n), BF16),
        compiler_params=_params("parallel", "arbitrary"),
        name="branch",
    )(o_a, o_b, o_c, w_a, w_b, w_c, gates, gates, gates)


def _outproj_kernel(m_ref, w_ref, x_ref, g_ref, x1_ref, u_ref):
    x1 = x_ref[...] + jnp.dot(m_ref[...], w_ref[...], preferred_element_type=F32)
    x1_ref[...] = x1
    ms = jnp.mean(x1 * x1, axis=-1, keepdims=True)
    u_ref[...] = (x1 * lax.rsqrt(ms + EPS) * g_ref[...]).astype(u_ref.dtype)


def _outproj(merged, w_out, layer, x, g, tm=512):
    s, d = x.shape
    return pl.pallas_call(
        _outproj_kernel,
        grid=(s // tm,),
        in_specs=[pl.BlockSpec((tm, d), lambda i: (i, 0)),
                  pl.BlockSpec((None, d, d), lambda i: (layer, 0, 0), pipeline_mode=pl.Buffered(1)),
                  pl.BlockSpec((tm, d), lambda i: (i, 0)),
                  pl.BlockSpec((1, d), lambda i: (0, 0))],
        out_specs=[pl.BlockSpec((tm, d), lambda i: (i, 0)),
                   pl.BlockSpec((tm, d), lambda i: (i, 0))],
        out_shape=[jax.ShapeDtypeStruct((s, d), F32), jax.ShapeDtypeStruct((s, d), BF16)],
        compiler_params=_params("parallel"),
        name="outproj",
    )(merged, w_out, x, g.reshape(1, d))


def _mlp_kernel(u_ref, wu_ref, wd_ref, x_ref, o_ref):
    f = pl.program_id(1)

    @pl.when(f == 0)
    def _():
        o_ref[...] = x_ref[...]

    h = jnp.dot(u_ref[...], wu_ref[...], preferred_element_type=F32)
    h = jnp.square(jnp.maximum(h, 0.0)).astype(BF16)
    o_ref[...] += jnp.dot(h, wd_ref[...], preferred_element_type=F32)


def _mlp(u, w_up, w_down, layer, x, tm=512, tf=1024):
    s, d = x.shape
    f = w_up.shape[2]
    return pl.pallas_call(
        _mlp_kernel,
        grid=(s // tm, f // tf),
        in_specs=[pl.BlockSpec((tm, d), lambda i, j: (i, 0)),
                  pl.BlockSpec((None, d, tf), lambda i, j: (layer, 0, j)),
                  pl.BlockSpec((None, tf, d), lambda i, j: (layer, j, 0)),
                  pl.BlockSpec((tm, d), lambda i, j: (i, 0))],
        out_specs=pl.BlockSpec((tm, d), lambda i, j: (i, 0)),
        out_shape=jax.ShapeDtypeStruct((s, d), F32),
        compiler_params=_params("parallel", "arbitrary"),
        name="mlp",
    )(u, w_up, w_down, x)


def _residue_major(t, dil):
    if dil == 1:
        return t
    s = t.shape[0]
    rest = t.shape[1:]
    return t.reshape((s // (BLOCK * dil), BLOCK, dil) + rest).swapaxes(1, 2).reshape((s,) + rest)


def _rope_tables(pos):
    half = ROPE_DIM // 2
    inv = ROPE_THETA ** (-jnp.arange(half, dtype=F32) / half)
    ang = pos.astype(F32)[:, None] * inv[None, :]
    cos, sin = jnp.cos(ang), jnp.sin(ang)
    n = pos.shape[0]
    pad = HEAD_DIM - ROPE_DIM
    t0 = jnp.concatenate([cos, cos, jnp.ones((n, pad), F32)], axis=1)
    t1 = jnp.concatenate([-sin, jnp.zeros((n, half + pad), F32)], axis=1)
    t2 = jnp.concatenate([jnp.zeros((n, half), F32), sin, jnp.zeros((n, pad), F32)], axis=1)
    return tuple(jnp.tile(t, (1, LANES // HEAD_DIM)) for t in (t0, t1, t2))


def _retention_rot_tables(pos):
    half = C_HEAD_DIM // 2
    inv = C_ROT_THETA ** (-jnp.arange(half, dtype=F32) / half)
    ang = pos.astype(F32)[:, None] * inv[None, :]
    cos, sin = jnp.cos(ang), jnp.sin(ang)
    t0 = jnp.concatenate([cos, cos], axis=1)
    t1 = jnp.concatenate([-sin, sin], axis=1)
    return t0, t1, t1


def _retention_decay_tables():
    c = RET_CHUNK
    log_g = jnp.log1p(-(2.0 ** (-5.0 - jnp.arange(C_HEADS, dtype=F32))))
    i = jnp.arange(c, dtype=F32)
    rel = i[:, None] - i[None, :]
    decay = jnp.where(rel >= 0, jnp.exp(log_g[:, None, None] * jnp.maximum(rel, 0.0)), 0.0)
    rows = lambda t: jnp.broadcast_to(t[:, :, None], (C_HEADS, c, C_HEAD_DIM))
    q_decay = rows(jnp.exp(log_g[:, None] * (i + 1.0)[None, :]))
    k_decay = rows(jnp.exp(log_g[:, None] * (c - 1 - i)[None, :]))
    chunk_decay = jnp.broadcast_to(jnp.exp(log_g * c)[:, None, None], (C_HEADS, C_HEAD_DIM, C_HEAD_DIM))
    return decay, q_decay, k_decay, chunk_decay


def _b_head_order():
    order = []
    for p in range(B_PAIRS):
        g2, a = divmod(p, B_PAIRS // 2)
        order += [B_REP * (2 * g2) + a, B_REP * (2 * g2 + 1) + a]
    return order


def _split_in(w):
    outs, start = [], 0
    for size in IN_SIZES:
        outs.append(w[..., start:start + size])
        start += size
    return outs


def _head_slices(w, order, axis):
    idx = [slice(None)] * w.ndim
    outs = []
    for h in order:
        idx[axis] = slice(h * HEAD_DIM, (h + 1) * HEAD_DIM)
        outs.append(w[tuple(idx)])
    return outs


def kernel(x, mix_norm, w_in, a_q_norm, a_k_norm, b_q_norm, b_k_norm, b_sinks, c_gn,
           w_br_a, w_br_b, w_br_c, w_out, mlp_norm, w_up, w_down):
    b, s, d = x.shape
    assert b == 1 and d == D_MODEL and s % A_SUPER == 0
    depth = w_in.shape[0]
    x = x.reshape(s, d)
    pos = jnp.arange(s)
    rope = [_rope_tables(_residue_major(pos, dil)) for _, dil in A_DILATED]
    rot_c = _retention_rot_tables(pos)
    decay_tabs = _retention_decay_tables()
    bd = (jnp.arange(LANES)[:, None] // HEAD_DIM == jnp.arange(LANES)[None, :] // HEAD_DIM).astype(BF16)
    b_order = _b_head_order()
    q_scale = HEAD_DIM ** -0.5 * LOG2E
    tile = jnp.tile

    aq, ak, av, bq, bk, bv, cq, ck, cv, cg, ga, gb, gc = _split_in(w_in)
    grp = lambda t, g: t[..., g * A_GROUP_WIDTH:(g + 1) * A_GROUP_WIDTH]
    w1152 = jnp.concatenate([grp(t, g) for g in range(A_GROUPS) for t in (aq, ak, av)] + [bv, cv, cg],
                            axis=-1).astype(BF16)
    w1024 = jnp.concatenate([cq, ck, ga, gb, gc], axis=-1).astype(BF16)
    w1280 = jnp.concatenate(_head_slices(bq, b_order, 2) + [bk], axis=-1).astype(BF16)
    wa = w_br_a.astype(BF16)
    wb = jnp.concatenate(_head_slices(w_br_b, b_order, 1), axis=1).astype(BF16)
    wc = w_br_c.astype(BF16)
    wo = w_out.astype(BF16)
    wu = w_up.astype(BF16)
    wd = w_down.astype(BF16)
    a_width = 3 * A_GROUP_WIDTH
    vm_width = B_KV_WIDTH + 2 * C_WIDTH

    us = _rmsnorm(x, mix_norm[0])
    for l in range(depth):
        vec = jnp.concatenate([tile(a_q_norm[l] * q_scale, A_HEADS_PER_GROUP),
                               tile(a_k_norm[l], A_HEADS_PER_GROUP),
                               jnp.ones((A_GROUP_WIDTH,), F32)])
        kinds = ["qk"] * (2 * A_PAIRS) + ["plain"] * A_PAIRS
        qkv = [_proj(us[g], w1152, l, g * a_width, a_width, kinds, tm=1024, tn=a_width,
                     vec=vec, tabs=rope[g], bd=bd) for g in range(A_GROUPS)]
        o_a = _attn_a(qkv)
        u = us[0]

        vec = jnp.concatenate([tile(b_q_norm[l] * q_scale, B_Q_HEADS), tile(b_k_norm[l], B_KV_HEADS)])
        bqk = _proj(u, w1280, l, 0, B_Q_WIDTH + B_KV_WIDTH, ["qk"] * (B_PAIRS + B_KV_HEADS // 2),
                    tm=1024, tn=B_Q_WIDTH + B_KV_WIDTH, vec=vec, tabs=rope[0], bd=bd)
        mode = jnp.concatenate([jnp.zeros((B_KV_WIDTH + C_WIDTH,), F32), jnp.ones((C_WIDTH,), F32)])
        vmisc = _proj(u, w1152, l, A_GROUPS * a_width, vm_width, ["act"] * (a_width // LANES),
                      tm=1024, tn=a_width, vec=mode)
        sinks = jnp.stack([b_sinks[l][h] for h in b_order]).astype(F32) * LOG2E
        o_b = _attn_b(bqk, vmisc, sinks)

        vec = jnp.concatenate([jnp.ones((C_WIDTH,), F32), jnp.full((C_WIDTH,), C_HEAD_DIM ** -0.5, F32)])
        cqk = _proj(u, w1024, l, 0, 2 * C_WIDTH, ["rot"] * 8, tm=1024, tn=1024, vec=vec, tabs=rot_c)
        o_c = _retention(cqk, vmisc, c_gn[l], decay_tabs)

        gates = _proj(u, w1024, l, 2 * C_WIDTH, 3 * D_MODEL, ["sigmoid"] * 8, tm=1024, tn=1024)
        merged = _branch(o_a, o_b, o_c, wa, wb, wc, l, gates)
        x, u2 = _outproj(merged, wo, l, x, mlp_norm[l])
        x = _mlp(u2, wu, wd, l, x)
        if l + 1 < depth:
            us = _rmsnorm(x, mix_norm[l + 1])
    return x.reshape(b, s, d)
```

```python
import functools
import math

import jax
import jax.numpy as jnp
from jax import lax
from jax.experimental import pallas as pl
from jax.experimental.pallas import tpu as pltpu

F32 = jnp.float32
BF16 = jnp.bfloat16

D_MODEL = 2048
EPS = 1e-6
NEG_INF = -1e30
BLOCK = 128
LANES = 128
HEAD_DIM = 64
ROPE_DIM = HEAD_DIM // 4
ROPE_THETA = 500000.0
A_DILATED = ((128, 1), (512, 4), (2048, 16))
A_GROUPS = len(A_DILATED)
A_HEADS_PER_GROUP = 6
A_GROUP_WIDTH = A_HEADS_PER_GROUP * HEAD_DIM
A_PAIRS = A_GROUP_WIDTH // LANES
A_SUPER = BLOCK * max(d for _, d in A_DILATED)
B_WINDOW = 128
B_Q_HEADS = 16
B_KV_HEADS = 4
B_REP = B_Q_HEADS // B_KV_HEADS
B_Q_WIDTH = B_Q_HEADS * HEAD_DIM
B_KV_WIDTH = B_KV_HEADS * HEAD_DIM
B_PAIRS = B_Q_WIDTH // LANES
C_HEADS = 8
C_HEAD_DIM = 128
C_WIDTH = C_HEADS * C_HEAD_DIM
C_ROT_THETA = 10000.0
RET_CHUNK = 256
RET_UNROLL = 8
D_FF = 4 * D_MODEL
IN_SIZES = (A_GROUPS * A_GROUP_WIDTH,) * 3 + (B_Q_WIDTH, B_KV_WIDTH, B_KV_WIDTH) + (C_WIDTH,) * 4 + (D_MODEL,) * 3

VMEM_LIMIT = 56 * 1024 * 1024
LOOP_UNROLL = 16
LOG2E = math.log2(math.e)
PROJ_SUB = 512


def _params(*sem):
    return pltpu.CompilerParams(dimension_semantics=sem, vmem_limit_bytes=VMEM_LIMIT)


def _sigmoid(a):
    return 1.0 / (1.0 + jnp.exp(-a))


def _rmsnorm_kernel(x_ref, g_ref, o_ref):
    x = x_ref[...]
    ms = jnp.mean(x * x, axis=-1, keepdims=True)
    o_ref[...] = (x * lax.rsqrt(ms + EPS) * g_ref[...]).astype(o_ref.dtype)


def _rmsnorm(x, g, tm=512):
    s, d = x.shape
    return pl.pallas_call(
        _rmsnorm_kernel,
        grid=(s // tm,),
        in_specs=[pl.BlockSpec((tm, d), lambda i: (i, 0)),
                  pl.BlockSpec((1, d), lambda i: (0, 0))],
        out_specs=pl.BlockSpec((tm, d), lambda i: (i, 0)),
        out_shape=jax.ShapeDtypeStruct((s, d), BF16),
        compiler_params=_params("parallel"),
        name="rmsnorm",
    )(x, g.reshape(1, d))


def _proj_kernel(kinds, sub, perm, *refs):
    refs = list(refs)
    perm_scr = refs.pop() if perm > 1 else None
    o_ref = refs.pop()
    u_ref, w_ref = refs[0], refs[1]
    rest = refs[2:]
    need_tab = any(k in ("qk", "rot") for k in kinds)
    vec_ref = rest.pop(0) if need_tab else None
    cos_ref, sin_ref = (rest.pop(0), rest.pop(0)) if need_tab else (None, None)
    bd_ref = rest.pop(0) if "qk" in kinds else None
    n_qk = sum(k == "qk" for k in kinds)
    assert n_qk % 2 == 0 and all(k == "qk" for k in kinds[:n_qk])
    dim = lax.broadcasted_iota(jnp.int32, (sub, LANES), 1) & (HEAD_DIM - 1)
    half = ROPE_DIM // 2
    chunk_rows = BLOCK * perm

    for m in range(u_ref.shape[0] // sub):
        rows = slice(m * sub, (m + 1) * sub)
        acc = jnp.dot(u_ref[rows, :], w_ref[...], preferred_element_type=F32)
        ss = []
        for c in range(n_qk // 2):
            a2 = acc[:, 2 * c * LANES:2 * (c + 1) * LANES]
            s2 = jnp.dot((a2 * a2).astype(BF16), bd_ref[...], preferred_element_type=F32)
            ss += [s2[:, :LANES], s2[:, LANES:]]
        for c, kind in enumerate(kinds):
            sl = slice(c * LANES, (c + 1) * LANES)
            a = acc[:, sl]
            if kind == "qk":
                y = a * lax.rsqrt(ss[c] * (1.0 / HEAD_DIM) + EPS) * vec_ref[:, sl]
                turned = jnp.where(dim < half, -pltpu.roll(y, LANES - half, 1),
                                   jnp.where(dim < ROPE_DIM, pltpu.roll(y, half, 1), 0.0))
                y = y * cos_ref[rows, :] + turned * sin_ref[rows, :]
            elif kind == "rot":
                y = (a * cos_ref[rows, :] + pltpu.roll(a, LANES // 2, 1) * sin_ref[rows, :]) * vec_ref[:, sl]
            elif kind == "sigmoid":
                y = _sigmoid(a)
            elif kind == "silu":
                y = a * _sigmoid(a)
            else:
                y = a
            if perm == 1:
                o_ref[rows, sl] = y.astype(o_ref.dtype)
            else:
                perm_scr[c] = y
                base = (m * sub) // chunk_rows * chunk_rows + (m * sub) % chunk_rows // perm
                n = sub // perm
                for r in range(perm):
                    o_ref[base + r * BLOCK:base + r * BLOCK + n, sl] = (
                        perm_scr[c, pl.ds(r, n, stride=perm), :].astype(o_ref.dtype))


def _proj(u, w, layer, col0, n, kinds, tm, tn, vec=None, tabs=None, bd=None, sub=PROJ_SUB, perm=1):
    s, k = u.shape
    assert n % tn == 0 and col0 % tn == 0 and s % tm == 0 and len(kinds) == tn // LANES
    assert tm % sub == 0
    assert perm == 1 or (sub % perm == 0 and (BLOCK * perm) % sub == 0 and tm % (BLOCK * perm) == 0)
    j0 = col0 // tn
    args = [u, w]
    in_specs = [pl.BlockSpec((tm, k), lambda i, j: (i, 0)),
                pl.BlockSpec((None, k, tn), lambda i, j: (layer, 0, j0 + j))]
    if tabs is not None:
        args.append(vec.reshape(1, n).astype(F32))
        in_specs.append(pl.BlockSpec((1, tn), lambda i, j: (0, j)))
        for t in tabs:
            args.append(t)
            in_specs.append(pl.BlockSpec((tm, LANES), lambda i, j: (i, 0)))
    if bd is not None:
        args.append(bd)
        in_specs.append(pl.BlockSpec(bd.shape, lambda i, j: (0, 0)))
    scratch = [pltpu.VMEM((len(kinds), sub, LANES), F32)] if perm > 1 else []
    return pl.pallas_call(
        functools.partial(_proj_kernel, tuple(kinds), sub, perm),
        grid=(s // tm, n // tn),
        in_specs=in_specs,
        out_specs=pl.BlockSpec((tm, tn), lambda i, j: (i, j)),
        out_shape=jax.ShapeDtypeStruct((s, n), BF16),
        scratch_shapes=scratch,
        compiler_params=_params("parallel", "arbitrary"),
        name="proj_" + kinds[0],
    )(*args)


def _half_masks(rows):
    lane = lax.broadcasted_iota(jnp.int32, (rows, LANES), 1)
    lo = jnp.where(lane < HEAD_DIM, 1.0, 0.0)
    return lo.astype(BF16), (1.0 - lo).astype(BF16)


def _pair_attend(q, k2, v2, bias, sink_key=False):
    if sink_key:
        rows = 16
        keep = jnp.where(lax.broadcasted_iota(jnp.int32, (rows, LANES), 0) > 0, 1.0, 0.0).astype(BF16)
        k2 = jnp.concatenate([k2[:rows] * keep, k2[rows:]], axis=0)
        v2 = jnp.concatenate([v2[:rows] * keep, v2[rows:]], axis=0)
    lo_b, hi_b = _half_masks(BLOCK)
    q2 = jnp.concatenate([q * lo_b, q * hi_b], axis=0)
    s2 = lax.dot_general(q2, k2, (((1,), (1,)), ((), ())), preferred_element_type=F32) + bias
    m2 = jnp.max(s2, axis=-1, keepdims=True)
    p2 = jnp.exp2(s2 - m2).astype(BF16)
    vx = jnp.concatenate([v2, jnp.ones_like(v2)], axis=1)
    r2 = jnp.dot(p2, vx, preferred_element_type=F32)
    lo = lax.broadcasted_iota(jnp.int32, (BLOCK, LANES), 1) < HEAD_DIM
    num = jnp.where(lo, r2[:BLOCK, :LANES], r2[BLOCK:, :LANES])
    den = jnp.where(lo, r2[:BLOCK, LANES:], r2[BLOCK:, LANES:])
    mx = jnp.where(lo, m2[:BLOCK], m2[BLOCK:])
    return num, den, mx


def _fill_bias(bias_ref, max_dist, sinks=None):
    row = lax.broadcasted_iota(jnp.int32, (2 * BLOCK, 2 * BLOCK), 0)
    kj = lax.broadcasted_iota(jnp.int32, (2 * BLOCK, 2 * BLOCK), 1)
    dist = (row & (BLOCK - 1)) + BLOCK - kj
    band = jnp.where(dist >= 0, jnp.where(dist <= max_dist, 0.0, NEG_INF), NEG_INF)
    first = jnp.where(kj >= BLOCK, band, NEG_INF)
    if sinks is not None:
        assert max_dist < BLOCK
        sink = jnp.where(row < BLOCK, sinks[0], sinks[1])
        band = jnp.where(kj == 0, sink, band)
        first = jnp.where(kj == 0, sink, first)
    bias_ref[0] = band
    bias_ref[1] = first


def _fill_ext(ext_ref, prev_ref, cur_ref):
    p = prev_ref.shape[0]
    ext_ref[0:p, :] = prev_ref[...]
    ext_ref[p:, :] = cur_ref[...]


def _attn_a_kernel(*refs):
    ins, (o_ref,), scr = refs[:15], refs[15:16], refs[16:]
    sb = pl.program_id(0)
    o_scr, l_scr, bias_ref = scr[0], scr[1], scr[2]
    _fill_bias(bias_ref, BLOCK)
    for g, (_, dil) in enumerate(A_DILATED):
        q_ref, k_ref, v_ref, kp_ref, vp_ref = ins[5 * g:5 * g + 5]
        kx_ref, vx_ref = scr[3 + 2 * g], scr[4 + 2 * g]
        _fill_ext(kx_ref, kp_ref, k_ref)
        _fill_ext(vx_ref, vp_ref, v_ref)
        prev_rows = BLOCK * dil

        def body(b, carry, g=g, dil=dil, q_ref=q_ref, kx_ref=kx_ref, vx_ref=vx_ref, prev_rows=prev_rows):
            row = pl.multiple_of(b * BLOCK, BLOCK)
            q = q_ref[pl.ds(row, BLOCK), :]
            k2 = jnp.concatenate([kx_ref[pl.ds(row, BLOCK), :],
                                  kx_ref[pl.ds(row + prev_rows, BLOCK), :]], axis=0)
            v2 = jnp.concatenate([vx_ref[pl.ds(row, BLOCK), :],
                                  vx_ref[pl.ds(row + prev_rows, BLOCK), :]], axis=0)
            no_prev = jnp.logical_and(sb == 0, b < dil)
            num, den, mx = _pair_attend(q, k2, v2, bias_ref[no_prev.astype(jnp.int32)])
            if dil == 1:
                dst = pl.ds(row, BLOCK)
            else:
                chunk = b // dil
                res = b - chunk * dil
                dst = pl.ds(chunk * prev_rows + res, BLOCK, stride=dil)
            o_scr[g, dst, :] = num / den
            l_scr[g, dst, :] = mx + jnp.log2(den)
            return carry

        lax.fori_loop(0, A_SUPER // BLOCK, body, 0, unroll=LOOP_UNROLL)

    l0, l1, l2 = l_scr[0], l_scr[1], l_scr[2]
    lm = jnp.maximum(jnp.maximum(l0, l1), l2)
    w0, w1, w2 = jnp.exp2(l0 - lm), jnp.exp2(l1 - lm), jnp.exp2(l2 - lm)
    o = (w0 * o_scr[0] + w1 * o_scr[1] + w2 * o_scr[2]) / (w0 + w1 + w2)
    o_ref[...] = o.astype(o_ref.dtype)


def _attn_a(qkv):
    s = qkv[0].shape[0]
    args, in_specs = [], []
    scratch = [pltpu.VMEM((A_GROUPS, A_SUPER, LANES), F32),
               pltpu.VMEM((A_GROUPS, A_SUPER, LANES), F32),
               pltpu.VMEM((2, 2 * BLOCK, 2 * BLOCK), F32)]
    for g, (_, dil) in enumerate(A_DILATED):
        prev_rows = BLOCK * dil
        per = A_SUPER // prev_rows

        def prev_map(off, per=per):
            return lambda i, p: (jnp.maximum(i * per - 1, 0), off + p)

        args += [qkv[g]] * 5
        in_specs += [pl.BlockSpec((A_SUPER, LANES), lambda i, p: (i, p)),
                     pl.BlockSpec((A_SUPER, LANES), lambda i, p: (i, A_PAIRS + p)),
                     pl.BlockSpec((A_SUPER, LANES), lambda i, p: (i, 2 * A_PAIRS + p)),
                     pl.BlockSpec((prev_rows, LANES), prev_map(A_PAIRS)),
                     pl.BlockSpec((prev_rows, LANES), prev_map(2 * A_PAIRS))]
        scratch += [pltpu.VMEM((prev_rows + A_SUPER, LANES), BF16)] * 2
    return pl.pallas_call(
        _attn_a_kernel,
        grid=(s // A_SUPER, A_PAIRS),
        in_specs=in_specs,
        out_specs=pl.BlockSpec((A_SUPER, LANES), lambda i, p: (i, p)),
        out_shape=jax.ShapeDtypeStruct((s, A_GROUP_WIDTH), BF16),
        scratch_shapes=scratch,
        compiler_params=_params("parallel", "parallel"),
        name="attn_a",
    )(*args)


def _attn_b_kernel(q_ref, k_ref, v_ref, kp_ref, vp_ref, sink_ref, o_ref, kx_ref, vx_ref, bias_ref):
    i = pl.program_id(0)
    p = pl.program_id(1)
    _fill_bias(bias_ref, B_WINDOW - 1, sinks=(sink_ref[2 * p], sink_ref[2 * p + 1]))
    _fill_ext(kx_ref, kp_ref, k_ref)
    _fill_ext(vx_ref, vp_ref, v_ref)

    def body(b, carry):
        row = pl.multiple_of(b * BLOCK, BLOCK)
        q = q_ref[pl.ds(row, BLOCK), :]
        k2 = kx_ref[pl.ds(row, 2 * BLOCK), :]
        v2 = vx_ref[pl.ds(row, 2 * BLOCK), :]
        no_prev = jnp.logical_and(i == 0, b == 0)
        num, den, _ = _pair_attend(q, k2, v2, bias_ref[no_prev.astype(jnp.int32)], sink_key=True)
        o_ref[pl.ds(row, BLOCK), :] = (num / den).astype(o_ref.dtype)
        return carry

    lax.fori_loop(0, q_ref.shape[0] // BLOCK, body, 0, unroll=LOOP_UNROLL)


def _attn_b(qk, vmisc, sinks, tq=2048):
    s = qk.shape[0]
    per = tq // BLOCK
    kv = lambda p: p // (B_PAIRS // 2)
    return pl.pallas_call(
        _attn_b_kernel,
        grid=(s // tq, B_PAIRS),
        in_specs=[pl.BlockSpec((tq, LANES), lambda i, p: (i, p)),
                  pl.BlockSpec((tq, LANES), lambda i, p: (i, B_PAIRS + kv(p))),
                  pl.BlockSpec((tq, LANES), lambda i, p: (i, kv(p))),
                  pl.BlockSpec((BLOCK, LANES), lambda i, p: (jnp.maximum(i * per - 1, 0), B_PAIRS + kv(p))),
                  pl.BlockSpec((BLOCK, LANES), lambda i, p: (jnp.maximum(i * per - 1, 0), kv(p))),
                  pl.BlockSpec(memory_space=pltpu.SMEM)],
        out_specs=pl.BlockSpec((tq, LANES), lambda i, p: (i, p)),
        out_shape=jax.ShapeDtypeStruct((s, B_Q_WIDTH), BF16),
        scratch_shapes=[pltpu.VMEM((BLOCK + tq, LANES), BF16)] * 2
                       + [pltpu.VMEM((2, 2 * BLOCK, 2 * BLOCK), F32)],
        compiler_params=_params("parallel", "parallel"),
        name="attn_b",
    )(qk, qk, vmisc, qk, vmisc, sinks)


def _retention_kernel(q_ref, k_ref, v_ref, g_ref, gn_ref, decay_ref, qd_ref, kd_ref, cd_ref, o_ref, state_ref):
    i = pl.program_id(0)
    h = pl.program_id(1)

    @pl.when(i == 0)
    def _():
        state_ref[h] = jnp.zeros((C_HEAD_DIM, C_HEAD_DIM), F32)

    gn = gn_ref[...]

    def body(n, state):
        row = pl.multiple_of(n * RET_CHUNK, RET_CHUNK)
        q = q_ref[pl.ds(row, RET_CHUNK), :]
        k = k_ref[pl.ds(row, RET_CHUNK), :]
        v = v_ref[pl.ds(row, RET_CHUNK), :]
        s = lax.dot_general(q, k, (((1,), (1,)), ((), ())), preferred_element_type=F32) * decay_ref[0]
        inner = jnp.dot(s.astype(BF16), v, preferred_element_type=F32)
        cross = jnp.dot(q, state.astype(BF16), preferred_element_type=F32) * qd_ref[0]
        kd_t = (k.astype(F32) * kd_ref[0]).T.astype(BF16)
        new_state = cd_ref[0] * state + jnp.dot(kd_t, v, preferred_element_type=F32)
        y = inner + cross
        y = y * lax.rsqrt(jnp.mean(y * y, axis=-1, keepdims=True) + EPS)
        o_ref[pl.ds(row, RET_CHUNK), :] = (g_ref[pl.ds(row, RET_CHUNK), :].astype(F32) * (y * gn)).astype(o_ref.dtype)
        return new_state

    state_ref[h] = lax.fori_loop(0, q_ref.shape[0] // RET_CHUNK, body, state_ref[h], unroll=RET_UNROLL)


def _retention(cqk, vmisc, c_gn, tabs, tq=2048):
    s = cqk.shape[0]
    v_off = B_KV_WIDTH // LANES
    g_off = v_off + C_HEADS
    tab = lambda t: pl.BlockSpec((1,) + t.shape[1:], lambda i, h: (h, 0, 0))
    return pl.pallas_call(
        _retention_kernel,
        grid=(s // tq, C_HEADS),
        in_specs=[pl.BlockSpec((tq, LANES), lambda i, h: (i, h)),
                  pl.BlockSpec((tq, LANES), lambda i, h: (i, C_HEADS + h)),
                  pl.BlockSpec((tq, LANES), lambda i, h: (i, v_off + h)),
                  pl.BlockSpec((tq, LANES), lambda i, h: (i, g_off + h)),
                  pl.BlockSpec((1, LANES), lambda i, h: (0, h))] + [tab(t) for t in tabs],
        out_specs=pl.BlockSpec((tq, LANES), lambda i, h: (i, h)),
        out_shape=jax.ShapeDtypeStruct((s, C_WIDTH), BF16),
        scratch_shapes=[pltpu.VMEM((C_HEADS, C_HEAD_DIM, C_HEAD_DIM), F32)],
        compiler_params=_params("arbitrary", "arbitrary"),
        name="retention",
    )(cqk, cqk, vmisc, vmisc, c_gn.reshape(1, C_WIDTH).astype(F32), *tabs)


def _branch_kernel(oa_ref, ob_ref, oc_ref, wa_ref, wb_ref, wc_ref, ga_ref, gb_ref, gc_ref, o_ref):
    a = jnp.dot(oa_ref[...], wa_ref[...], preferred_element_type=F32)
    m = ga_ref[...].astype(F32) * a
    b = jnp.dot(ob_ref[...], wb_ref[...], preferred_element_type=F32)
    m = m + gb_ref[...].astype(F32) * b
    c = jnp.dot(oc_ref[...], wc_ref[...], preferred_element_type=F32)
    m = m + gc_ref[...].astype(F32) * c
    o_ref[...] = m.astype(o_ref.dtype)


def _branch(o_a, o_b, o_c, w_a, w_b, w_c, layer, gates, tm=1024, tn=1024):
    s = o_a.shape[0]
    n = w_a.shape[2]
    nt = n // tn
    act = lambda width: pl.BlockSpec((tm, width), lambda i, j: (i, 0))
    wgt = lambda width: pl.BlockSpec((None, width, tn), lambda i, j: (layer, 0, j))
    gate = lambda off: pl.BlockSpec((tm, tn), lambda i, j: (i, off * nt + j))
    return pl.pallas_call(
        _branch_kernel,
        grid=(s // tm, nt),
        in_specs=[act(o_a.shape[1]), act(o_b.shape[1]), act(o_c.shape[1]),
                  wgt(w_a.shape[1]), wgt(w_b.shape[1]), wgt(w_c.shape[1]),
                  gate(0), gate(1), gate(2)],
        out_specs=pl.BlockSpec((tm, tn), lambda i, j: (i, j)),
        out_shape=jax.ShapeDtypeStruct((s, n), BF16),
        compiler_params=_params("parallel", "arbitrary"),
        name="branch",
    )(o_a, o_b, o_c, w_a, w_b, w_c, gates, gates, gates)


def _outproj_kernel(m_ref, w_ref, x_ref, g_ref, x1_ref, u_ref):
    x1 = x_ref[...] + jnp.dot(m_ref[...], w_ref[...], preferred_element_type=F32)
    x1_ref[...] = x1
    ms = jnp.mean(x1 * x1, axis=-1, keepdims=True)
    u_ref[...] = (x1 * lax.rsqrt(ms + EPS) * g_ref[...]).astype(u_ref.dtype)


def _outproj(merged, w_out, layer, x, g, tm=512):
    s, d = x.shape
    return pl.pallas_call(
        _outproj_kernel,
        grid=(s // tm,),
        in_specs=[pl.BlockSpec((tm, d), lambda i: (i, 0)),
                  pl.BlockSpec((None, d, d), lambda i: (layer, 0, 0), pipeline_mode=pl.Buffered(1)),
                  pl.BlockSpec((tm, d), lambda i: (i, 0)),
                  pl.BlockSpec((1, d), lambda i: (0, 0))],
        out_specs=[pl.BlockSpec((tm, d), lambda i: (i, 0)),
                   pl.BlockSpec((tm, d), lambda i: (i, 0))],
        out_shape=[jax.ShapeDtypeStruct((s, d), F32), jax.ShapeDtypeStruct((s, d), BF16)],
        compiler_params=_params("parallel"),
        name="outproj",
    )(merged, w_out, x, g.reshape(1, d))


def _mlp_kernel(u_ref, wu_ref, wd_ref, x_ref, o_ref):
    f = pl.program_id(1)

    @pl.when(f == 0)
    def _():
        o_ref[...] = x_ref[...]

    h = jnp.dot(u_ref[...], wu_ref[...], preferred_element_type=F32)
    h = jnp.square(jnp.maximum(h, 0.0)).astype(BF16)
    o_ref[...] += jnp.dot(h, wd_ref[...], preferred_element_type=F32)


def _mlp(u, w_up, w_down, layer, x, tm=512, tf=1024):
    s, d = x.shape
    f = w_up.shape[2]
    return pl.pallas_call(
        _mlp_kernel,
        grid=(s // tm, f // tf),
        in_specs=[pl.BlockSpec((tm, d), lambda i, j: (i, 0)),
                  pl.BlockSpec((None, d, tf), lambda i, j: (layer, 0, j)),
                  pl.BlockSpec((None, tf, d), lambda i, j: (layer, j, 0)),
                  pl.BlockSpec((tm, d), lambda i, j: (i, 0))],
        out_specs=pl.BlockSpec((tm, d), lambda i, j: (i, 0)),
        out_shape=jax.ShapeDtypeStruct((s, d), F32),
        compiler_params=_params("parallel", "arbitrary"),
        name="mlp",
    )(u, w_up, w_down, x)


def _rope_tables(pos):
    half = ROPE_DIM // 2
    inv = ROPE_THETA ** (-jnp.arange(half, dtype=F32) / half)
    ang = pos.astype(F32)[:, None] * inv[None, :]
    cos, sin = jnp.cos(ang), jnp.sin(ang)
    n = pos.shape[0]
    pad = HEAD_DIM - ROPE_DIM
    t_cos = jnp.concatenate([cos, cos, jnp.ones((n, pad), F32)], axis=1)
    t_sin = jnp.concatenate([sin, sin, jnp.zeros((n, pad), F32)], axis=1)
    return tuple(jnp.tile(t, (1, LANES // HEAD_DIM)) for t in (t_cos, t_sin))


def _retention_rot_tables(pos):
    half = C_HEAD_DIM // 2
    inv = C_ROT_THETA ** (-jnp.arange(half, dtype=F32) / half)
    ang = pos.astype(F32)[:, None] * inv[None, :]
    cos, sin = jnp.cos(ang), jnp.sin(ang)
    return jnp.concatenate([cos, cos], axis=1), jnp.concatenate([-sin, sin], axis=1)


def _retention_decay_tables():
    c = RET_CHUNK
    log_g = jnp.log1p(-(2.0 ** (-5.0 - jnp.arange(C_HEADS, dtype=F32))))
    i = jnp.arange(c, dtype=F32)
    rel = i[:, None] - i[None, :]
    decay = jnp.where(rel >= 0, jnp.exp(log_g[:, None, None] * jnp.maximum(rel, 0.0)), 0.0)
    rows = lambda t: jnp.broadcast_to(t[:, :, None], (C_HEADS, c, C_HEAD_DIM))
    q_decay = rows(jnp.exp(log_g[:, None] * (i + 1.0)[None, :]))
    k_decay = rows(jnp.exp(log_g[:, None] * (c - 1 - i)[None, :]))
    chunk_decay = jnp.broadcast_to(jnp.exp(log_g * c)[:, None, None], (C_HEADS, C_HEAD_DIM, C_HEAD_DIM))
    return decay, q_decay, k_decay, chunk_decay


def _b_head_order():
    order = []
    for p in range(B_PAIRS):
        g2, a = divmod(p, B_PAIRS // 2)
        order += [B_REP * (2 * g2) + a, B_REP * (2 * g2 + 1) + a]
    return order


def _split_in(w):
    outs, start = [], 0
    for size in IN_SIZES:
        outs.append(w[..., start:start + size])
        start += size
    return outs


def _head_slices(w, order, axis):
    idx = [slice(None)] * w.ndim
    outs = []
    for h in order:
        idx[axis] = slice(h * HEAD_DIM, (h + 1) * HEAD_DIM)
        outs.append(w[tuple(idx)])
    return outs


def kernel(x, mix_norm, w_in, a_q_norm, a_k_norm, b_q_norm, b_k_norm, b_sinks, c_gn,
           w_br_a, w_br_b, w_br_c, w_out, mlp_norm, w_up, w_down):
    b, s, d = x.shape
    assert b == 1 and d == D_MODEL and s % A_SUPER == 0
    depth = w_in.shape[0]
    x = x.reshape(s, d)
    pos = jnp.arange(s)
    rope = _rope_tables(pos)
    rot_c = _retention_rot_tables(pos)
    decay_tabs = _retention_decay_tables()
    width2 = 2 * LANES
    bd = (jnp.arange(width2)[:, None] // HEAD_DIM == jnp.arange(width2)[None, :] // HEAD_DIM).astype(BF16)
    b_order = _b_head_order()
    q_scale = HEAD_DIM ** -0.5 * LOG2E
    tile = jnp.tile

    aq, ak, av, bq, bk, bv, cq, ck, cv, cg, ga, gb, gc = _split_in(w_in)
    grp = lambda t, g: t[..., g * A_GROUP_WIDTH:(g + 1) * A_GROUP_WIDTH]
    w1152 = jnp.concatenate([grp(t, g) for g in range(A_GROUPS) for t in (aq, ak, av)], axis=-1).astype(BF16)
    w2304 = jnp.concatenate([bv, cv, cg], axis=-1).astype(BF16)
    w1024 = jnp.concatenate([cq, ck, ga, gb, gc], axis=-1).astype(BF16)
    w1280 = jnp.concatenate(_head_slices(bq, b_order, 2) + [bk], axis=-1).astype(BF16)
    wa = w_br_a.astype(BF16)
    wb = jnp.concatenate(_head_slices(w_br_b, b_order, 1), axis=1).astype(BF16)
    wc = w_br_c.astype(BF16)
    wo = w_out.astype(BF16)
    wu = w_up.astype(BF16)
    wd = w_down.astype(BF16)
    a_width = 3 * A_GROUP_WIDTH
    vm_width = B_KV_WIDTH + 2 * C_WIDTH

    u = _rmsnorm(x, mix_norm[0])
    for l in range(depth):
        vec = jnp.concatenate([tile(a_q_norm[l] * q_scale, A_HEADS_PER_GROUP),
                               tile(a_k_norm[l], A_HEADS_PER_GROUP),
                               jnp.ones((A_GROUP_WIDTH,), F32)])
        kinds = ["qk"] * (2 * A_PAIRS) + ["plain"] * A_PAIRS
        qkv = [_proj(u, w1152, l, g * a_width, a_width, kinds, tm=A_SUPER, tn=a_width,
                     vec=vec, tabs=rope, bd=bd, perm=dil) for g, (_, dil) in enumerate(A_DILATED)]
        o_a = _attn_a(qkv)

        vec = jnp.concatenate([tile(b_q_norm[l] * q_scale, B_Q_HEADS), tile(b_k_norm[l], B_KV_HEADS)])
        bqk = _proj(u, w1280, l, 0, B_Q_WIDTH + B_KV_WIDTH, ["qk"] * (B_PAIRS + B_KV_HEADS // 2),
                    tm=1024, tn=B_Q_WIDTH + B_KV_WIDTH, vec=vec, tabs=rope, bd=bd)
        kinds = ["plain"] * ((B_KV_WIDTH + C_WIDTH) // LANES) + ["silu"] * (C_WIDTH // LANES)
        vmisc = _proj(u, w2304, l, 0, vm_width, kinds, tm=1024, tn=vm_width)
        sinks = jnp.stack([b_sinks[l][h] for h in b_order]).astype(F32) * LOG2E
        o_b = _attn_b(bqk, vmisc, sinks)

        vec = jnp.concatenate([jnp.ones((C_WIDTH,), F32), jnp.full((C_WIDTH,), C_HEAD_DIM ** -0.5, F32)])
        cqk = _proj(u, w1024, l, 0, 2 * C_WIDTH, ["rot"] * 8, tm=2048, tn=1024, vec=vec, tabs=rot_c)
        o_c = _retention(cqk, vmisc, c_gn[l], decay_tabs)

        gates = _proj(u, w1024, l, 2 * C_WIDTH, 3 * D_MODEL, ["sigmoid"] * 8, tm=2048, tn=1024)
        merged = _branch(o_a, o_b, o_c, wa, wb, wc, l, gates)
        x, u2 = _outproj(merged, wo, l, x, mlp_norm[l])
        x = _mlp(u2, wu, wd, l, x)
        if l + 1 < depth:
            u = _rmsnorm(x, mix_norm[l + 1])
    return x.reshape(b, s, d)
```

```python
import functools
import math

import jax
import jax.numpy as jnp
from jax import lax
from jax.experimental import pallas as pl
from jax.experimental.pallas import tpu as pltpu

F32 = jnp.float32
BF16 = jnp.bfloat16

D_MODEL = 2048
EPS = 1e-6
NEG_INF = -1e30
BLOCK = 128
LANES = 128
HEAD_DIM = 64
ROPE_DIM = HEAD_DIM // 4
ROPE_THETA = 500000.0
A_DILATED = ((128, 1), (512, 4), (2048, 16))
A_GROUPS = len(A_DILATED)
A_HEADS_PER_GROUP = 6
A_GROUP_WIDTH = A_HEADS_PER_GROUP * HEAD_DIM
A_PAIRS = A_GROUP_WIDTH // LANES
A_SUPER = BLOCK * max(d for _, d in A_DILATED)
B_WINDOW = 128
B_Q_HEADS = 16
B_KV_HEADS = 4
B_REP = B_Q_HEADS // B_KV_HEADS
B_Q_WIDTH = B_Q_HEADS * HEAD_DIM
B_KV_WIDTH = B_KV_HEADS * HEAD_DIM
B_PAIRS = B_Q_WIDTH // LANES
C_HEADS = 8
C_HEAD_DIM = 128
C_WIDTH = C_HEADS * C_HEAD_DIM
C_ROT_THETA = 10000.0
RET_CHUNK = 256
RET_UNROLL = 8
D_FF = 4 * D_MODEL
IN_SIZES = (A_GROUPS * A_GROUP_WIDTH,) * 3 + (B_Q_WIDTH, B_KV_WIDTH, B_KV_WIDTH) + (C_WIDTH,) * 4 + (D_MODEL,) * 3

VMEM_LIMIT = 56 * 1024 * 1024
LOOP_UNROLL = 16
LOG2E = math.log2(math.e)
PROJ_SUB = 512


def _params(*sem):
    return pltpu.CompilerParams(dimension_semantics=sem, vmem_limit_bytes=VMEM_LIMIT)


def _sigmoid(a):
    return 1.0 / (1.0 + jnp.exp(-a))


def _rmsnorm_kernel(x_ref, g_ref, o_ref):
    x = x_ref[...]
    ms = jnp.mean(x * x, axis=-1, keepdims=True)
    o_ref[...] = (x * lax.rsqrt(ms + EPS) * g_ref[...]).astype(o_ref.dtype)


def _rmsnorm(x, g, tm=512):
    s, d = x.shape
    return pl.pallas_call(
        _rmsnorm_kernel,
        grid=(s // tm,),
        in_specs=[pl.BlockSpec((tm, d), lambda i: (i, 0)),
                  pl.BlockSpec((1, d), lambda i: (0, 0))],
        out_specs=pl.BlockSpec((tm, d), lambda i: (i, 0)),
        out_shape=jax.ShapeDtypeStruct((s, d), BF16),
        compiler_params=_params("parallel"),
        name="rmsnorm",
    )(x, g.reshape(1, d))


def _proj_kernel(kinds, sub, perm, *refs):
    refs = list(refs)
    perm_scr = refs.pop() if perm > 1 else None
    o_ref = refs.pop()
    u_ref, w_ref = refs[0], refs[1]
    rest = refs[2:]
    need_tab = any(k in ("qk", "rot") for k in kinds)
    vec_ref = rest.pop(0) if need_tab else None
    cos_ref, sin_ref = (rest.pop(0), rest.pop(0)) if need_tab else (None, None)
    bd_ref = rest.pop(0) if "qk" in kinds else None
    n_qk = sum(k == "qk" for k in kinds)
    assert n_qk % 2 == 0 and all(k == "qk" for k in kinds[:n_qk])
    dim = lax.broadcasted_iota(jnp.int32, (sub, LANES), 1) & (HEAD_DIM - 1)
    half = ROPE_DIM // 2
    chunk_rows = BLOCK * perm

    for m in range(u_ref.shape[0] // sub):
        rows = slice(m * sub, (m + 1) * sub)
        acc = jnp.dot(u_ref[rows, :], w_ref[...], preferred_element_type=F32)
        ss = []
        for c in range(n_qk // 2):
            a2 = acc[:, 2 * c * LANES:2 * (c + 1) * LANES]
            s2 = jnp.dot((a2 * a2).astype(BF16), bd_ref[...], preferred_element_type=F32)
            ss += [s2[:, :LANES], s2[:, LANES:]]
        for c, kind in enumerate(kinds):
            sl = slice(c * LANES, (c + 1) * LANES)
            a = acc[:, sl]
            if kind == "qk":
                y = a * lax.rsqrt(ss[c] * (1.0 / HEAD_DIM) + EPS) * vec_ref[:, sl]
                turned = jnp.where(dim < half, -pltpu.roll(y, LANES - half, 1),
                                   jnp.where(dim < ROPE_DIM, pltpu.roll(y, half, 1), 0.0))
                y = y * cos_ref[rows, :] + turned * sin_ref[rows, :]
            elif kind == "rot":
                y = (a * cos_ref[rows, :] + pltpu.roll(a, LANES // 2, 1) * sin_ref[rows, :]) * vec_ref[:, sl]
            elif kind == "sigmoid":
                y = _sigmoid(a)
            elif kind == "silu":
                y = a * _sigmoid(a)
            elif kind == "relu2":
                y = jnp.square(jnp.maximum(a, 0.0))
            else:
                y = a
            if perm == 1:
                o_ref[rows, sl] = y.astype(o_ref.dtype)
            else:
                perm_scr[c] = y
                base = (m * sub) // chunk_rows * chunk_rows + (m * sub) % chunk_rows // perm
                n = sub // perm
                for r in range(perm):
                    o_ref[base + r * BLOCK:base + r * BLOCK + n, sl] = (
                        perm_scr[c, pl.ds(r, n, stride=perm), :].astype(o_ref.dtype))


def _proj(u, w, layer, col0, n, kinds, tm, tn, vec=None, tabs=None, bd=None, sub=PROJ_SUB, perm=1):
    s, k = u.shape
    assert n % tn == 0 and col0 % tn == 0 and s % tm == 0 and len(kinds) == tn // LANES
    assert tm % sub == 0
    assert perm == 1 or (sub % perm == 0 and (BLOCK * perm) % sub == 0 and tm % (BLOCK * perm) == 0)
    j0 = col0 // tn
    args = [u, w]
    in_specs = [pl.BlockSpec((tm, k), lambda i, j: (i, 0)),
                pl.BlockSpec((None, k, tn), lambda i, j: (layer, 0, j0 + j))]
    if tabs is not None:
        args.append(vec.reshape(1, n).astype(F32))
        in_specs.append(pl.BlockSpec((1, tn), lambda i, j: (0, j)))
        for t in tabs:
            args.append(t)
            in_specs.append(pl.BlockSpec((tm, LANES), lambda i, j: (i, 0)))
    if bd is not None:
        args.append(bd)
        in_specs.append(pl.BlockSpec(bd.shape, lambda i, j: (0, 0)))
    scratch = [pltpu.VMEM((len(kinds), sub, LANES), F32)] if perm > 1 else []
    return pl.pallas_call(
        functools.partial(_proj_kernel, tuple(kinds), sub, perm),
        grid=(s // tm, n // tn),
        in_specs=in_specs,
        out_specs=pl.BlockSpec((tm, tn), lambda i, j: (i, j)),
        out_shape=jax.ShapeDtypeStruct((s, n), BF16),
        scratch_shapes=scratch,
        compiler_params=_params("parallel", "arbitrary"),
        name="proj_" + kinds[0],
    )(*args)


def _half_masks(rows):
    lane = lax.broadcasted_iota(jnp.int32, (rows, LANES), 1)
    lo = jnp.where(lane < HEAD_DIM, 1.0, 0.0)
    return lo.astype(BF16), (1.0 - lo).astype(BF16)


def _pair_attend(q, k2, v2, bias, sink_key=False):
    if sink_key:
        rows = 16
        keep = jnp.where(lax.broadcasted_iota(jnp.int32, (rows, LANES), 0) > 0, 1.0, 0.0).astype(BF16)
        k2 = jnp.concatenate([k2[:rows] * keep, k2[rows:]], axis=0)
        v2 = jnp.concatenate([v2[:rows] * keep, v2[rows:]], axis=0)
    lo_b, hi_b = _half_masks(BLOCK)
    q2 = jnp.concatenate([q * lo_b, q * hi_b], axis=0)
    s2 = lax.dot_general(q2, k2, (((1,), (1,)), ((), ())), preferred_element_type=F32) + bias
    m2 = jnp.max(s2, axis=-1, keepdims=True)
    p2 = jnp.exp2(s2 - m2).astype(BF16)
    vx = jnp.concatenate([v2, jnp.ones_like(v2)], axis=1)
    r2 = jnp.dot(p2, vx, preferred_element_type=F32)
    lo = lax.broadcasted_iota(jnp.int32, (BLOCK, LANES), 1) < HEAD_DIM
    num = jnp.where(lo, r2[:BLOCK, :LANES], r2[BLOCK:, :LANES])
    den = jnp.where(lo, r2[:BLOCK, LANES:], r2[BLOCK:, LANES:])
    mx = jnp.where(lo, m2[:BLOCK], m2[BLOCK:])
    return num, den, mx


def _fill_bias(bias_ref, max_dist, sinks=None):
    row = lax.broadcasted_iota(jnp.int32, (2 * BLOCK, 2 * BLOCK), 0)
    kj = lax.broadcasted_iota(jnp.int32, (2 * BLOCK, 2 * BLOCK), 1)
    dist = (row & (BLOCK - 1)) + BLOCK - kj
    band = jnp.where(dist >= 0, jnp.where(dist <= max_dist, 0.0, NEG_INF), NEG_INF)
    first = jnp.where(kj >= BLOCK, band, NEG_INF)
    if sinks is not None:
        assert max_dist < BLOCK
        sink = jnp.where(row < BLOCK, sinks[0], sinks[1])
        band = jnp.where(kj == 0, sink, band)
        first = jnp.where(kj == 0, sink, first)
    bias_ref[0] = band
    bias_ref[1] = first


def _fill_ext(ext_ref, prev_ref, cur_ref):
    p = prev_ref.shape[0]
    ext_ref[0:p, :] = prev_ref[...]
    ext_ref[p:, :] = cur_ref[...]


def _attn_a_kernel(*refs):
    ins, (o_ref,), scr = refs[:15], refs[15:16], refs[16:]
    sb = pl.program_id(0)
    o_scr, l_scr, bias_ref = scr[0], scr[1], scr[2]
    _fill_bias(bias_ref, BLOCK)
    for g, (_, dil) in enumerate(A_DILATED):
        q_ref, k_ref, v_ref, kp_ref, vp_ref = ins[5 * g:5 * g + 5]
        kx_ref, vx_ref = scr[3 + 2 * g], scr[4 + 2 * g]
        _fill_ext(kx_ref, kp_ref, k_ref)
        _fill_ext(vx_ref, vp_ref, v_ref)
        prev_rows = BLOCK * dil

        def body(b, carry, g=g, dil=dil, q_ref=q_ref, kx_ref=kx_ref, vx_ref=vx_ref, prev_rows=prev_rows):
            row = pl.multiple_of(b * BLOCK, BLOCK)
            q = q_ref[pl.ds(row, BLOCK), :]
            k2 = jnp.concatenate([kx_ref[pl.ds(row, BLOCK), :],
                                  kx_ref[pl.ds(row + prev_rows, BLOCK), :]], axis=0)
            v2 = jnp.concatenate([vx_ref[pl.ds(row, BLOCK), :],
                                  vx_ref[pl.ds(row + prev_rows, BLOCK), :]], axis=0)
            no_prev = jnp.logical_and(sb == 0, b < dil)
            num, den, mx = _pair_attend(q, k2, v2, bias_ref[no_prev.astype(jnp.int32)])
            if dil == 1:
                dst = pl.ds(row, BLOCK)
            else:
                chunk = b // dil
                res = b - chunk * dil
                dst = pl.ds(chunk * prev_rows + res, BLOCK, stride=dil)
            o_scr[g, dst, :] = num / den
            l_scr[g, dst, :] = mx + jnp.log2(den)
            return carry

        lax.fori_loop(0, A_SUPER // BLOCK, body, 0, unroll=LOOP_UNROLL)

    l0, l1, l2 = l_scr[0], l_scr[1], l_scr[2]
    lm = jnp.maximum(jnp.maximum(l0, l1), l2)
    w0, w1, w2 = jnp.exp2(l0 - lm), jnp.exp2(l1 - lm), jnp.exp2(l2 - lm)
    o = (w0 * o_scr[0] + w1 * o_scr[1] + w2 * o_scr[2]) / (w0 + w1 + w2)
    o_ref[...] = o.astype(o_ref.dtype)


def _attn_a(qkv):
    s = qkv[0].shape[0]
    args, in_specs = [], []
    scratch = [pltpu.VMEM((A_GROUPS, A_SUPER, LANES), F32),
               pltpu.VMEM((A_GROUPS, A_SUPER, LANES), F32),
               pltpu.VMEM((2, 2 * BLOCK, 2 * BLOCK), F32)]
    for g, (_, dil) in enumerate(A_DILATED):
        prev_rows = BLOCK * dil
        per = A_SUPER // prev_rows

        def prev_map(off, per=per):
            return lambda i, p: (jnp.maximum(i * per - 1, 0), off + p)

        args += [qkv[g]] * 5
        in_specs += [pl.BlockSpec((A_SUPER, LANES), lambda i, p: (i, p)),
                     pl.BlockSpec((A_SUPER, LANES), lambda i, p: (i, A_PAIRS + p)),
                     pl.BlockSpec((A_SUPER, LANES), lambda i, p: (i, 2 * A_PAIRS + p)),
                     pl.BlockSpec((prev_rows, LANES), prev_map(A_PAIRS)),
                     pl.BlockSpec((prev_rows, LANES), prev_map(2 * A_PAIRS))]
        scratch += [pltpu.VMEM((prev_rows + A_SUPER, LANES), BF16)] * 2
    return pl.pallas_call(
        _attn_a_kernel,
        grid=(s // A_SUPER, A_PAIRS),
        in_specs=in_specs,
        out_specs=pl.BlockSpec((A_SUPER, LANES), lambda i, p: (i, p)),
        out_shape=jax.ShapeDtypeStruct((s, A_GROUP_WIDTH), BF16),
        scratch_shapes=scratch,
        compiler_params=_params("parallel", "parallel"),
        name="attn_a",
    )(*args)


def _attn_b_kernel(q_ref, k_ref, v_ref, kp_ref, vp_ref, sink_ref, o_ref, kx_ref, vx_ref, bias_ref):
    i = pl.program_id(0)
    p = pl.program_id(1)
    _fill_bias(bias_ref, B_WINDOW - 1, sinks=(sink_ref[2 * p], sink_ref[2 * p + 1]))
    _fill_ext(kx_ref, kp_ref, k_ref)
    _fill_ext(vx_ref, vp_ref, v_ref)

    def body(b, carry):
        row = pl.multiple_of(b * BLOCK, BLOCK)
        q = q_ref[pl.ds(row, BLOCK), :]
        k2 = kx_ref[pl.ds(row, 2 * BLOCK), :]
        v2 = vx_ref[pl.ds(row, 2 * BLOCK), :]
        no_prev = jnp.logical_and(i == 0, b == 0)
        num, den, _ = _pair_attend(q, k2, v2, bias_ref[no_prev.astype(jnp.int32)], sink_key=True)
        o_ref[pl.ds(row, BLOCK), :] = (num / den).astype(o_ref.dtype)
        return carry

    lax.fori_loop(0, q_ref.shape[0] // BLOCK, body, 0, unroll=LOOP_UNROLL)


def _attn_b(qk, vmisc, sinks, tq=2048):
    s = qk.shape[0]
    per = tq // BLOCK
    kv = lambda p: p // (B_PAIRS // 2)
    return pl.pallas_call(
        _attn_b_kernel,
        grid=(s // tq, B_PAIRS),
        in_specs=[pl.BlockSpec((tq, LANES), lambda i, p: (i, p)),
                  pl.BlockSpec((tq, LANES), lambda i, p: (i, B_PAIRS + kv(p))),
                  pl.BlockSpec((tq, LANES), lambda i, p: (i, kv(p))),
                  pl.BlockSpec((BLOCK, LANES), lambda i, p: (jnp.maximum(i * per - 1, 0), B_PAIRS + kv(p))),
                  pl.BlockSpec((BLOCK, LANES), lambda i, p: (jnp.maximum(i * per - 1, 0), kv(p))),
                  pl.BlockSpec(memory_space=pltpu.SMEM)],
        out_specs=pl.BlockSpec((tq, LANES), lambda i, p: (i, p)),
        out_shape=jax.ShapeDtypeStruct((s, B_Q_WIDTH), BF16),
        scratch_shapes=[pltpu.VMEM((BLOCK + tq, LANES), BF16)] * 2
                       + [pltpu.VMEM((2, 2 * BLOCK, 2 * BLOCK), F32)],
        compiler_params=_params("parallel", "parallel"),
        name="attn_b",
    )(qk, qk, vmisc, qk, vmisc, sinks)


def _retention_kernel(q_ref, k_ref, v_ref, g_ref, gn_ref, decay_ref, qd_ref, kd_ref, cd_ref, o_ref, state_ref):
    i = pl.program_id(0)
    h = pl.program_id(1)

    @pl.when(i == 0)
    def _():
        state_ref[h] = jnp.zeros((C_HEAD_DIM, C_HEAD_DIM), F32)

    gn = gn_ref[...]

    def body(n, state):
        row = pl.multiple_of(n * RET_CHUNK, RET_CHUNK)
        q = q_ref[pl.ds(row, RET_CHUNK), :]
        k = k_ref[pl.ds(row, RET_CHUNK), :]
        v = v_ref[pl.ds(row, RET_CHUNK), :]
        s = lax.dot_general(q, k, (((1,), (1,)), ((), ())), preferred_element_type=F32) * decay_ref[0]
        inner = jnp.dot(s.astype(BF16), v, preferred_element_type=F32)
        cross = jnp.dot(q, state.astype(BF16), preferred_element_type=F32) * qd_ref[0]
        kd_t = (k.astype(F32) * kd_ref[0]).T.astype(BF16)
        new_state = cd_ref[0] * state + jnp.dot(kd_t, v, preferred_element_type=F32)
        y = inner + cross
        y = y * lax.rsqrt(jnp.mean(y * y, axis=-1, keepdims=True) + EPS)
        o_ref[pl.ds(row, RET_CHUNK), :] = (g_ref[pl.ds(row, RET_CHUNK), :].astype(F32) * (y * gn)).astype(o_ref.dtype)
        return new_state

    state_ref[h] = lax.fori_loop(0, q_ref.shape[0] // RET_CHUNK, body, state_ref[h], unroll=RET_UNROLL)


def _retention(cqk, vmisc, c_gn, tabs, tq=2048):
    s = cqk.shape[0]
    v_off = B_KV_WIDTH // LANES
    g_off = v_off + C_HEADS
    tab = lambda t: pl.BlockSpec((1,) + t.shape[1:], lambda i, h: (h, 0, 0))
    return pl.pallas_call(
        _retention_kernel,
        grid=(s // tq, C_HEADS),
        in_specs=[pl.BlockSpec((tq, LANES), lambda i, h: (i, h)),
                  pl.BlockSpec((tq, LANES), lambda i, h: (i, C_HEADS + h)),
                  pl.BlockSpec((tq, LANES), lambda i, h: (i, v_off + h)),
                  pl.BlockSpec((tq, LANES), lambda i, h: (i, g_off + h)),
                  pl.BlockSpec((1, LANES), lambda i, h: (0, h))] + [tab(t) for t in tabs],
        out_specs=pl.BlockSpec((tq, LANES), lambda i, h: (i, h)),
        out_shape=jax.ShapeDtypeStruct((s, C_WIDTH), BF16),
        scratch_shapes=[pltpu.VMEM((C_HEADS, C_HEAD_DIM, C_HEAD_DIM), F32)],
        compiler_params=_params("arbitrary", "arbitrary"),
        name="retention",
    )(cqk, cqk, vmisc, vmisc, c_gn.reshape(1, C_WIDTH).astype(F32), *tabs)


def _branch_kernel(sub, u_ref, ga_ref, gb_ref, gc_ref, oa_ref, ob_ref, oc_ref, wa_ref, wb_ref, wc_ref, o_ref):
    for m in range(u_ref.shape[0] // sub):
        rows = slice(m * sub, (m + 1) * sub)
        u = u_ref[rows, :]
        total = None
        for g_ref, o_in, w_ref in ((ga_ref, oa_ref, wa_ref), (gb_ref, ob_ref, wb_ref), (gc_ref, oc_ref, wc_ref)):
            gate = _sigmoid(jnp.dot(u, g_ref[...], preferred_element_type=F32))
            term = gate * jnp.dot(o_in[rows, :], w_ref[...], preferred_element_type=F32)
            total = term if total is None else total + term
        o_ref[rows, :] = total.astype(o_ref.dtype)


def _branch(u, w_in_pad, gate_cols, o_a, o_b, o_c, w_a, w_b, w_c, layer, tm=1024, tn=512, sub=PROJ_SUB):
    s, k = u.shape
    n = w_a.shape[2]
    act = lambda width: pl.BlockSpec((tm, width), lambda i, j: (i, 0))
    wgt = lambda width: pl.BlockSpec((None, width, tn), lambda i, j: (layer, 0, j))
    assert all(c % tn == 0 for c in gate_cols)
    gate = lambda col: pl.BlockSpec((None, k, tn), lambda i, j: (layer, 0, col // tn + j))
    return pl.pallas_call(
        functools.partial(_branch_kernel, sub),
        grid=(s // tm, n // tn),
        in_specs=[act(k)] + [gate(c) for c in gate_cols]
                 + [act(o_a.shape[1]), act(o_b.shape[1]), act(o_c.shape[1]),
                    wgt(w_a.shape[1]), wgt(w_b.shape[1]), wgt(w_c.shape[1])],
        out_specs=pl.BlockSpec((tm, tn), lambda i, j: (i, j)),
        out_shape=jax.ShapeDtypeStruct((s, n), BF16),
        compiler_params=_params("parallel", "arbitrary"),
        name="branch",
    )(u, w_in_pad, w_in_pad, w_in_pad, o_a, o_b, o_c, w_a, w_b, w_c)


def _outproj_kernel(m_ref, w_ref, x_ref, g_ref, x1_ref, u_ref):
    x1 = x_ref[...] + jnp.dot(m_ref[...], w_ref[...], preferred_element_type=F32)
    x1_ref[...] = x1
    ms = jnp.mean(x1 * x1, axis=-1, keepdims=True)
    u_ref[...] = (x1 * lax.rsqrt(ms + EPS) * g_ref[...]).astype(u_ref.dtype)


def _outproj(merged, w_out, layer, x, g, tm=512):
    s, d = x.shape
    return pl.pallas_call(
        _outproj_kernel,
        grid=(s // tm,),
        in_specs=[pl.BlockSpec((tm, d), lambda i: (i, 0)),
                  pl.BlockSpec((None, d, d), lambda i: (layer, 0, 0), pipeline_mode=pl.Buffered(1)),
                  pl.BlockSpec((tm, d), lambda i: (i, 0)),
                  pl.BlockSpec((1, d), lambda i: (0, 0))],
        out_specs=[pl.BlockSpec((tm, d), lambda i: (i, 0)),
                   pl.BlockSpec((tm, d), lambda i: (i, 0))],
        out_shape=[jax.ShapeDtypeStruct((s, d), F32), jax.ShapeDtypeStruct((s, d), BF16)],
        compiler_params=_params("parallel"),
        name="outproj",
    )(merged, w_out, x, g.reshape(1, d))


def _mlp_down_kernel(sub, h_ref, w_ref, x_ref, o_ref):
    for m in range(h_ref.shape[0] // sub):
        rows = slice(m * sub, (m + 1) * sub)
        o_ref[rows, :] = x_ref[rows, :] + jnp.dot(h_ref[rows, :], w_ref[...], preferred_element_type=F32)


def _mlp_down(h, w_down, layer, x, tm=1024, tn=256, sub=PROJ_SUB):
    s, d = x.shape
    f = h.shape[1]
    return pl.pallas_call(
        functools.partial(_mlp_down_kernel, sub),
        grid=(s // tm, d // tn),
        in_specs=[pl.BlockSpec((tm, f), lambda i, j: (i, 0)),
                  pl.BlockSpec((None, f, tn), lambda i, j: (layer, 0, j)),
                  pl.BlockSpec((tm, tn), lambda i, j: (i, j))],
        out_specs=pl.BlockSpec((tm, tn), lambda i, j: (i, j)),
        out_shape=jax.ShapeDtypeStruct((s, d), F32),
        compiler_params=_params("parallel", "arbitrary"),
        name="mlp_down",
    )(h, w_down, x)


def _rope_tables(pos):
    half = ROPE_DIM // 2
    inv = ROPE_THETA ** (-jnp.arange(half, dtype=F32) / half)
    ang = pos.astype(F32)[:, None] * inv[None, :]
    cos, sin = jnp.cos(ang), jnp.sin(ang)
    n = pos.shape[0]
    pad = HEAD_DIM - ROPE_DIM
    t_cos = jnp.concatenate([cos, cos, jnp.ones((n, pad), F32)], axis=1)
    t_sin = jnp.concatenate([sin, sin, jnp.zeros((n, pad), F32)], axis=1)
    return tuple(jnp.tile(t, (1, LANES // HEAD_DIM)) for t in (t_cos, t_sin))


def _retention_rot_tables(pos):
    half = C_HEAD_DIM // 2
    inv = C_ROT_THETA ** (-jnp.arange(half, dtype=F32) / half)
    ang = pos.astype(F32)[:, None] * inv[None, :]
    cos, sin = jnp.cos(ang), jnp.sin(ang)
    return jnp.concatenate([cos, cos], axis=1), jnp.concatenate([-sin, sin], axis=1)


def _retention_decay_tables():
    c = RET_CHUNK
    log_g = jnp.log1p(-(2.0 ** (-5.0 - jnp.arange(C_HEADS, dtype=F32))))
    i = jnp.arange(c, dtype=F32)
    rel = i[:, None] - i[None, :]
    decay = jnp.where(rel >= 0, jnp.exp(log_g[:, None, None] * jnp.maximum(rel, 0.0)), 0.0)
    rows = lambda t: jnp.broadcast_to(t[:, :, None], (C_HEADS, c, C_HEAD_DIM))
    q_decay = rows(jnp.exp(log_g[:, None] * (i + 1.0)[None, :]))
    k_decay = rows(jnp.exp(log_g[:, None] * (c - 1 - i)[None, :]))
    chunk_decay = jnp.broadcast_to(jnp.exp(log_g * c)[:, None, None], (C_HEADS, C_HEAD_DIM, C_HEAD_DIM))
    return decay, q_decay, k_decay, chunk_decay


def _b_head_order():
    order = []
    for p in range(B_PAIRS):
        g2, a = divmod(p, B_PAIRS // 2)
        order += [B_REP * (2 * g2) + a, B_REP * (2 * g2 + 1) + a]
    return order


def _split_in(w):
    outs, start = [], 0
    for size in IN_SIZES:
        outs.append(w[..., start:start + size])
        start += size
    return outs


def _head_slices(w, order, axis):
    idx = [slice(None)] * w.ndim
    outs = []
    for h in order:
        idx[axis] = slice(h * HEAD_DIM, (h + 1) * HEAD_DIM)
        outs.append(w[tuple(idx)])
    return outs


def kernel(x, mix_norm, w_in, a_q_norm, a_k_norm, b_q_norm, b_k_norm, b_sinks, c_gn,
           w_br_a, w_br_b, w_br_c, w_out, mlp_norm, w_up, w_down):
    b, s, d = x.shape
    assert b == 1 and d == D_MODEL and s % A_SUPER == 0
    depth = w_in.shape[0]
    x = x.reshape(s, d)
    pos = jnp.arange(s)
    rope = _rope_tables(pos)
    rot_c = _retention_rot_tables(pos)
    decay_tabs = _retention_decay_tables()
    width2 = 2 * LANES
    bd = (jnp.arange(width2)[:, None] // HEAD_DIM == jnp.arange(width2)[None, :] // HEAD_DIM).astype(BF16)
    b_order = _b_head_order()
    q_scale = HEAD_DIM ** -0.5 * LOG2E
    tile = jnp.tile

    w_pad = jnp.pad(w_in, ((0, 0), (0, 0), (LANES, 0))).astype(BF16)
    starts = [LANES + sum(IN_SIZES[:i]) for i in range(len(IN_SIZES))]
    cq_col, gate_cols = starts[6], starts[10:13]
    aq, ak, av, bq, bk, bv, cq, ck, cv, cg, ga, gb, gc = _split_in(w_pad[..., LANES:])
    grp = lambda t, g: t[..., g * A_GROUP_WIDTH:(g + 1) * A_GROUP_WIDTH]
    w1152 = jnp.concatenate([grp(t, g) for g in range(A_GROUPS) for t in (aq, ak, av)], axis=-1)
    w2304 = jnp.concatenate([bv, cv, cg], axis=-1)
    w1280 = jnp.concatenate(_head_slices(bq, b_order, 2) + [bk], axis=-1)
    wa = w_br_a.astype(BF16)
    wb = jnp.concatenate(_head_slices(w_br_b, b_order, 1), axis=1).astype(BF16)
    wc = w_br_c.astype(BF16)
    wo = w_out.astype(BF16)
    wu = w_up.astype(BF16)
    wd = w_down.astype(BF16)
    a_width = 3 * A_GROUP_WIDTH
    vm_width = B_KV_WIDTH + 2 * C_WIDTH

    u = _rmsnorm(x, mix_norm[0])
    for l in range(depth):
        vec = jnp.concatenate([tile(a_q_norm[l] * q_scale, A_HEADS_PER_GROUP),
                               tile(a_k_norm[l], A_HEADS_PER_GROUP),
                               jnp.ones((A_GROUP_WIDTH,), F32)])
        kinds = ["qk"] * (2 * A_PAIRS) + ["plain"] * A_PAIRS
        qkv = [_proj(u, w1152, l, g * a_width, a_width, kinds, tm=A_SUPER, tn=a_width,
                     vec=vec, tabs=rope, bd=bd, perm=dil) for g, (_, dil) in enumerate(A_DILATED)]
        o_a = _attn_a(qkv)

        vec = jnp.concatenate([tile(b_q_norm[l] * q_scale, B_Q_HEADS), tile(b_k_norm[l], B_KV_HEADS)])
        bqk = _proj(u, w1280, l, 0, B_Q_WIDTH + B_KV_WIDTH, ["qk"] * (B_PAIRS + B_KV_HEADS // 2),
                    tm=1024, tn=B_Q_WIDTH + B_KV_WIDTH, vec=vec, tabs=rope, bd=bd)
        kinds = ["plain"] * ((B_KV_WIDTH + C_WIDTH) // LANES) + ["silu"] * (C_WIDTH // LANES)
        vmisc = _proj(u, w2304, l, 0, vm_width, kinds, tm=1024, tn=vm_width)
        sinks = jnp.stack([b_sinks[l][h] for h in b_order]).astype(F32) * LOG2E
        o_b = _attn_b(bqk, vmisc, sinks)

        vec = jnp.concatenate([jnp.ones((C_WIDTH,), F32), jnp.full((C_WIDTH,), C_HEAD_DIM ** -0.5, F32)])
        cqk = _proj(u, w_pad, l, cq_col, 2 * C_WIDTH, ["rot"] * 8, tm=2048, tn=1024, vec=vec, tabs=rot_c)
        o_c = _retention(cqk, vmisc, c_gn[l], decay_tabs)

        merged = _branch(u, w_pad, gate_cols, o_a, o_b, o_c, wa, wb, wc, l)
        x, u2 = _outproj(merged, wo, l, x, mlp_norm[l])
        hidden = _proj(u2, wu, l, 0, D_FF, ["relu2"] * 8, tm=2048, tn=1024)
        x = _mlp_down(hidden, wd, l, x)
        if l + 1 < depth:
            u = _rmsnorm(x, mix_norm[l + 1])
    return x.reshape(b, s, d)
```

```python
import functools
import math

import jax
import jax.numpy as jnp
import numpy as np
from jax import lax
from jax.experimental import pallas as pl
from jax.experimental.pallas import tpu as pltpu

F32 = jnp.float32
BF16 = jnp.bfloat16

D_MODEL = 2048
EPS = 1e-6
NEG_INF = -1e30
BLOCK = 128
LANES = 128
HEAD_DIM = 64
ROPE_DIM = HEAD_DIM // 4
ROPE_THETA = 500000.0
A_DILATED = ((128, 1), (512, 4), (2048, 16))
A_GROUPS = len(A_DILATED)
A_HEADS_PER_GROUP = 6
A_GROUP_WIDTH = A_HEADS_PER_GROUP * HEAD_DIM
A_PAIRS = A_GROUP_WIDTH // LANES
A_SUPER = BLOCK * max(d for _, d in A_DILATED)
B_WINDOW = 128
B_Q_HEADS = 16
B_KV_HEADS = 4
B_REP = B_Q_HEADS // B_KV_HEADS
B_Q_WIDTH = B_Q_HEADS * HEAD_DIM
B_KV_WIDTH = B_KV_HEADS * HEAD_DIM
B_PAIRS = B_Q_WIDTH // LANES
C_HEADS = 8
C_HEAD_DIM = 128
C_WIDTH = C_HEADS * C_HEAD_DIM
C_ROT_THETA = 10000.0
RET_CHUNK = 256
RET_UNROLL = 8
D_FF = 4 * D_MODEL
IN_SIZES = (A_GROUPS * A_GROUP_WIDTH,) * 3 + (B_Q_WIDTH, B_KV_WIDTH, B_KV_WIDTH) + (C_WIDTH,) * 4 + (D_MODEL,) * 3

VMEM_LIMIT = 56 * 1024 * 1024
LOOP_UNROLL = 16
LOG2E = math.log2(math.e)
PROJ_SUB = 512


def _params(*sem):
    return pltpu.CompilerParams(dimension_semantics=sem, vmem_limit_bytes=VMEM_LIMIT)


def _sigmoid(a):
    return 1.0 / (1.0 + jnp.exp(-a))


def _rmsnorm_kernel(x_ref, g_ref, o_ref):
    x = x_ref[...]
    ms = jnp.mean(x * x, axis=-1, keepdims=True)
    o_ref[...] = (x * lax.rsqrt(ms + EPS) * g_ref[...]).astype(o_ref.dtype)


def _rmsnorm(x, g, tm=512):
    s, d = x.shape
    return pl.pallas_call(
        _rmsnorm_kernel,
        grid=(s // tm,),
        in_specs=[pl.BlockSpec((tm, d), lambda i: (i, 0)),
                  pl.BlockSpec((1, d), lambda i: (0, 0))],
        out_specs=pl.BlockSpec((tm, d), lambda i: (i, 0)),
        out_shape=jax.ShapeDtypeStruct((s, d), BF16),
        compiler_params=_params("parallel"),
        name="rmsnorm",
    )(x, g.reshape(1, d))


def _proj_kernel(kinds, sub, perm, *refs):
    refs = list(refs)
    perm_scr = refs.pop() if perm > 1 else None
    o_ref = refs.pop()
    u_ref, w_ref = refs[0], refs[1]
    rest = refs[2:]
    need_tab = any(k in ("qk", "rot") for k in kinds)
    vec_ref = rest.pop(0) if need_tab else None
    cos_ref, sin_ref = (rest.pop(0), rest.pop(0)) if need_tab else (None, None)
    bd_ref = rest.pop(0) if "qk" in kinds else None
    n_qk = sum(k == "qk" for k in kinds)
    assert n_qk % 2 == 0 and all(k == "qk" for k in kinds[:n_qk])
    dim = lax.broadcasted_iota(jnp.int32, (sub, LANES), 1) & (HEAD_DIM - 1)
    half = ROPE_DIM // 2
    chunk_rows = BLOCK * perm

    for m in range(u_ref.shape[0] // sub):
        rows = slice(m * sub, (m + 1) * sub)
        acc = jnp.dot(u_ref[rows, :], w_ref[...], preferred_element_type=F32)
        ss = []
        for c in range(n_qk // 2):
            a2 = acc[:, 2 * c * LANES:2 * (c + 1) * LANES]
            s2 = jnp.dot((a2 * a2).astype(BF16), bd_ref[...], preferred_element_type=F32)
            ss += [s2[:, :LANES], s2[:, LANES:]]
        for c, kind in enumerate(kinds):
            sl = slice(c * LANES, (c + 1) * LANES)
            a = acc[:, sl]
            if kind == "qk":
                y = a * lax.rsqrt(ss[c] * (1.0 / HEAD_DIM) + EPS) * vec_ref[:, sl]
                turned = jnp.where(dim < half, -pltpu.roll(y, LANES - half, 1),
                                   jnp.where(dim < ROPE_DIM, pltpu.roll(y, half, 1), 0.0))
                y = y * cos_ref[rows, :] + turned * sin_ref[rows, :]
            elif kind == "rot":
                y = (a * cos_ref[rows, :] + pltpu.roll(a, LANES // 2, 1) * sin_ref[rows, :]) * vec_ref[:, sl]
            elif kind == "sigmoid":
                y = _sigmoid(a)
            elif kind == "silu":
                y = a * _sigmoid(a)
            elif kind == "relu2":
                y = jnp.square(jnp.maximum(a, 0.0))
            else:
                y = a
            if perm == 1:
                o_ref[rows, sl] = y.astype(o_ref.dtype)
            else:
                perm_scr[c] = y
                base = (m * sub) // chunk_rows * chunk_rows + (m * sub) % chunk_rows // perm
                n = sub // perm
                for r in range(perm):
                    o_ref[base + r * BLOCK:base + r * BLOCK + n, sl] = (
                        perm_scr[c, pl.ds(r, n, stride=perm), :].astype(o_ref.dtype))


def _proj(u, w, layer, col0, n, kinds, tm, tn, vec=None, tabs=None, bd=None, sub=PROJ_SUB, perm=1):
    s, k = u.shape
    assert n % tn == 0 and col0 % tn == 0 and s % tm == 0 and len(kinds) == tn // LANES
    assert tm % sub == 0
    assert perm == 1 or (sub % perm == 0 and (BLOCK * perm) % sub == 0 and tm % (BLOCK * perm) == 0)
    j0 = col0 // tn
    args = [u, w]
    in_specs = [pl.BlockSpec((tm, k), lambda i, j: (i, 0)),
                pl.BlockSpec((None, k, tn), lambda i, j: (layer, 0, j0 + j))]
    if tabs is not None:
        args.append(vec.reshape(1, n).astype(F32))
        in_specs.append(pl.BlockSpec((1, tn), lambda i, j: (0, j)))
        for t in tabs:
            args.append(t)
            in_specs.append(pl.BlockSpec((tm, LANES), lambda i, j: (i, 0)))
    if bd is not None:
        args.append(bd)
        in_specs.append(pl.BlockSpec(bd.shape, lambda i, j: (0, 0)))
    scratch = [pltpu.VMEM((len(kinds), sub, LANES), F32)] if perm > 1 else []
    return pl.pallas_call(
        functools.partial(_proj_kernel, tuple(kinds), sub, perm),
        grid=(s // tm, n // tn),
        in_specs=in_specs,
        out_specs=pl.BlockSpec((tm, tn), lambda i, j: (i, j)),
        out_shape=jax.ShapeDtypeStruct((s, n), BF16),
        scratch_shapes=scratch,
        compiler_params=_params("parallel", "arbitrary"),
        name="proj_" + kinds[0],
    )(*args)


def _half_masks(rows):
    lane = lax.broadcasted_iota(jnp.int32, (rows, LANES), 1)
    lo = jnp.where(lane < HEAD_DIM, 1.0, 0.0)
    return lo.astype(BF16), (1.0 - lo).astype(BF16)


def _pair_attend(q, k2, v2, bias, sink_key=False):
    if sink_key:
        rows = 16
        keep = jnp.where(lax.broadcasted_iota(jnp.int32, (rows, LANES), 0) > 0, 1.0, 0.0).astype(BF16)
        k2 = jnp.concatenate([k2[:rows] * keep, k2[rows:]], axis=0)
        v2 = jnp.concatenate([v2[:rows] * keep, v2[rows:]], axis=0)
    lo_b, hi_b = _half_masks(BLOCK)
    q2 = jnp.concatenate([q * lo_b, q * hi_b], axis=0)
    s2 = lax.dot_general(q2, k2, (((1,), (1,)), ((), ())), preferred_element_type=F32) + bias
    m2 = jnp.max(s2, axis=-1, keepdims=True)
    p2 = jnp.exp2(s2 - m2).astype(BF16)
    vx = jnp.concatenate([v2, jnp.ones_like(v2)], axis=1)
    r2 = jnp.dot(p2, vx, preferred_element_type=F32)
    lo = lax.broadcasted_iota(jnp.int32, (BLOCK, LANES), 1) < HEAD_DIM
    num = jnp.where(lo, r2[:BLOCK, :LANES], r2[BLOCK:, :LANES])
    den = jnp.where(lo, r2[:BLOCK, LANES:], r2[BLOCK:, LANES:])
    mx = jnp.where(lo, m2[:BLOCK], m2[BLOCK:])
    return num, den, mx


def _fill_bias(bias_ref, max_dist, sinks=None):
    row = lax.broadcasted_iota(jnp.int32, (2 * BLOCK, 2 * BLOCK), 0)
    kj = lax.broadcasted_iota(jnp.int32, (2 * BLOCK, 2 * BLOCK), 1)
    dist = (row & (BLOCK - 1)) + BLOCK - kj
    band = jnp.where(dist >= 0, jnp.where(dist <= max_dist, 0.0, NEG_INF), NEG_INF)
    first = jnp.where(kj >= BLOCK, band, NEG_INF)
    if sinks is not None:
        assert max_dist < BLOCK
        sink = jnp.where(row < BLOCK, sinks[0], sinks[1])
        band = jnp.where(kj == 0, sink, band)
        first = jnp.where(kj == 0, sink, first)
    bias_ref[0] = band
    bias_ref[1] = first


def _fill_ext(ext_ref, prev_ref, cur_ref):
    p = prev_ref.shape[0]
    ext_ref[0:p, :] = prev_ref[...]
    ext_ref[p:, :] = cur_ref[...]


def _attn_a_kernel(*refs):
    ins, (o_ref,), scr = refs[:15], refs[15:16], refs[16:]
    sb = pl.program_id(0)
    o_scr, l_scr, bias_ref = scr[0], scr[1], scr[2]
    _fill_bias(bias_ref, BLOCK)
    for g, (_, dil) in enumerate(A_DILATED):
        q_ref, k_ref, v_ref, kp_ref, vp_ref = ins[5 * g:5 * g + 5]
        kx_ref, vx_ref = scr[3 + 2 * g], scr[4 + 2 * g]
        _fill_ext(kx_ref, kp_ref, k_ref)
        _fill_ext(vx_ref, vp_ref, v_ref)
        prev_rows = BLOCK * dil

        def body(b, carry, g=g, dil=dil, q_ref=q_ref, kx_ref=kx_ref, vx_ref=vx_ref, prev_rows=prev_rows):
            row = pl.multiple_of(b * BLOCK, BLOCK)
            q = q_ref[pl.ds(row, BLOCK), :]
            k2 = jnp.concatenate([kx_ref[pl.ds(row, BLOCK), :],
                                  kx_ref[pl.ds(row + prev_rows, BLOCK), :]], axis=0)
            v2 = jnp.concatenate([vx_ref[pl.ds(row, BLOCK), :],
                                  vx_ref[pl.ds(row + prev_rows, BLOCK), :]], axis=0)
            no_prev = jnp.logical_and(sb == 0, b < dil)
            num, den, mx = _pair_attend(q, k2, v2, bias_ref[no_prev.astype(jnp.int32)])
            if dil == 1:
                dst = pl.ds(row, BLOCK)
            else:
                chunk = b // dil
                res = b - chunk * dil
                dst = pl.ds(chunk * prev_rows + res, BLOCK, stride=dil)
            o_scr[g, dst, :] = num / den
            l_scr[g, dst, :] = mx + jnp.log2(den)
            return carry

        lax.fori_loop(0, A_SUPER // BLOCK, body, 0, unroll=LOOP_UNROLL)

    l0, l1, l2 = l_scr[0], l_scr[1], l_scr[2]
    lm = jnp.maximum(jnp.maximum(l0, l1), l2)
    w0, w1, w2 = jnp.exp2(l0 - lm), jnp.exp2(l1 - lm), jnp.exp2(l2 - lm)
    o = (w0 * o_scr[0] + w1 * o_scr[1] + w2 * o_scr[2]) / (w0 + w1 + w2)
    o_ref[...] = o.astype(o_ref.dtype)


def _attn_a(qkv):
    s = qkv[0].shape[0]
    args, in_specs = [], []
    scratch = [pltpu.VMEM((A_GROUPS, A_SUPER, LANES), F32),
               pltpu.VMEM((A_GROUPS, A_SUPER, LANES), F32),
               pltpu.VMEM((2, 2 * BLOCK, 2 * BLOCK), F32)]
    for g, (_, dil) in enumerate(A_DILATED):
        prev_rows = BLOCK * dil
        per = A_SUPER // prev_rows

        def prev_map(off, per=per):
            return lambda i, p: (jnp.maximum(i * per - 1, 0), off + p)

        args += [qkv[g]] * 5
        in_specs += [pl.BlockSpec((A_SUPER, LANES), lambda i, p: (i, p)),
                     pl.BlockSpec((A_SUPER, LANES), lambda i, p: (i, A_PAIRS + p)),
                     pl.BlockSpec((A_SUPER, LANES), lambda i, p: (i, 2 * A_PAIRS + p)),
                     pl.BlockSpec((prev_rows, LANES), prev_map(A_PAIRS)),
                     pl.BlockSpec((prev_rows, LANES), prev_map(2 * A_PAIRS))]
        scratch += [pltpu.VMEM((prev_rows + A_SUPER, LANES), BF16)] * 2
    return pl.pallas_call(
        _attn_a_kernel,
        grid=(s // A_SUPER, A_PAIRS),
        in_specs=in_specs,
        out_specs=pl.BlockSpec((A_SUPER, LANES), lambda i, p: (i, p)),
        out_shape=jax.ShapeDtypeStruct((s, A_GROUP_WIDTH), BF16),
        scratch_shapes=scratch,
        compiler_params=_params("parallel", "parallel"),
        name="attn_a",
    )(*args)


def _attn_b_kernel(q_ref, k_ref, v_ref, kp_ref, vp_ref, sink_ref, o_ref, kx_ref, vx_ref, bias_ref):
    i = pl.program_id(0)
    p = pl.program_id(1)
    _fill_bias(bias_ref, B_WINDOW - 1, sinks=(sink_ref[2 * p], sink_ref[2 * p + 1]))
    _fill_ext(kx_ref, kp_ref, k_ref)
    _fill_ext(vx_ref, vp_ref, v_ref)

    def body(b, carry):
        row = pl.multiple_of(b * BLOCK, BLOCK)
        q = q_ref[pl.ds(row, BLOCK), :]
        k2 = kx_ref[pl.ds(row, 2 * BLOCK), :]
        v2 = vx_ref[pl.ds(row, 2 * BLOCK), :]
        no_prev = jnp.logical_and(i == 0, b == 0)
        num, den, _ = _pair_attend(q, k2, v2, bias_ref[no_prev.astype(jnp.int32)], sink_key=True)
        o_ref[pl.ds(row, BLOCK), :] = (num / den).astype(o_ref.dtype)
        return carry

    lax.fori_loop(0, q_ref.shape[0] // BLOCK, body, 0, unroll=LOOP_UNROLL)


def _attn_b(qkv, sinks, tq=2048):
    s = qkv.shape[0]
    per = tq // BLOCK
    kv = lambda p: p // (B_PAIRS // 2)
    k0 = B_PAIRS
    v0 = B_PAIRS + B_KV_WIDTH // LANES
    return pl.pallas_call(
        _attn_b_kernel,
        grid=(s // tq, B_PAIRS),
        in_specs=[pl.BlockSpec((tq, LANES), lambda i, p: (i, p)),
                  pl.BlockSpec((tq, LANES), lambda i, p: (i, k0 + kv(p))),
                  pl.BlockSpec((tq, LANES), lambda i, p: (i, v0 + kv(p))),
                  pl.BlockSpec((BLOCK, LANES), lambda i, p: (jnp.maximum(i * per - 1, 0), k0 + kv(p))),
                  pl.BlockSpec((BLOCK, LANES), lambda i, p: (jnp.maximum(i * per - 1, 0), v0 + kv(p))),
                  pl.BlockSpec(memory_space=pltpu.SMEM)],
        out_specs=pl.BlockSpec((tq, LANES), lambda i, p: (i, p)),
        out_shape=jax.ShapeDtypeStruct((s, B_Q_WIDTH), BF16),
        scratch_shapes=[pltpu.VMEM((BLOCK + tq, LANES), BF16)] * 2
                       + [pltpu.VMEM((2, 2 * BLOCK, 2 * BLOCK), F32)],
        compiler_params=_params("parallel", "parallel"),
        name="attn_b",
    )(qkv, qkv, qkv, qkv, qkv, sinks)


def _retention_kernel(q_ref, k_ref, v_ref, g_ref, gn_ref, decay_ref, qd_ref, kd_ref, cd_ref, o_ref, state_ref):
    i = pl.program_id(0)
    h = pl.program_id(1)

    @pl.when(i == 0)
    def _():
        state_ref[h] = jnp.zeros((C_HEAD_DIM, C_HEAD_DIM), F32)

    gn = gn_ref[...]

    def body(n, state):
        row = pl.multiple_of(n * RET_CHUNK, RET_CHUNK)
        q = q_ref[pl.ds(row, RET_CHUNK), :]
        k = k_ref[pl.ds(row, RET_CHUNK), :]
        v = v_ref[pl.ds(row, RET_CHUNK), :]
        s = lax.dot_general(q, k, (((1,), (1,)), ((), ())), preferred_element_type=F32) * decay_ref[0]
        inner = jnp.dot(s.astype(BF16), v, preferred_element_type=F32)
        cross = jnp.dot(q, state.astype(BF16), preferred_element_type=F32) * qd_ref[0]
        kd_t = (k.astype(F32) * kd_ref[0]).T.astype(BF16)
        new_state = cd_ref[0] * state + jnp.dot(kd_t, v, preferred_element_type=F32)
        y = inner + cross
        y = y * lax.rsqrt(jnp.mean(y * y, axis=-1, keepdims=True) + EPS)
        o_ref[pl.ds(row, RET_CHUNK), :] = (g_ref[pl.ds(row, RET_CHUNK), :].astype(F32) * (y * gn)).astype(o_ref.dtype)
        return new_state

    state_ref[h] = lax.fori_loop(0, q_ref.shape[0] // RET_CHUNK, body, state_ref[h], unroll=RET_UNROLL)


def _retention(cqk, cvg, c_gn, tabs, tq=2048):
    s = cqk.shape[0]
    v_off = 0
    g_off = C_HEADS
    tab = lambda t: pl.BlockSpec((1,) + t.shape[1:], lambda i, h: (h, 0, 0))
    return pl.pallas_call(
        _retention_kernel,
        grid=(s // tq, C_HEADS),
        in_specs=[pl.BlockSpec((tq, LANES), lambda i, h: (i, h)),
                  pl.BlockSpec((tq, LANES), lambda i, h: (i, C_HEADS + h)),
                  pl.BlockSpec((tq, LANES), lambda i, h: (i, v_off + h)),
                  pl.BlockSpec((tq, LANES), lambda i, h: (i, g_off + h)),
                  pl.BlockSpec((1, LANES), lambda i, h: (0, h))] + [tab(t) for t in tabs],
        out_specs=pl.BlockSpec((tq, LANES), lambda i, h: (i, h)),
        out_shape=jax.ShapeDtypeStruct((s, C_WIDTH), BF16),
        scratch_shapes=[pltpu.VMEM((C_HEADS, C_HEAD_DIM, C_HEAD_DIM), F32)],
        compiler_params=_params("arbitrary", "arbitrary"),
        name="retention",
    )(cqk, cqk, cvg, cvg, c_gn.reshape(1, C_WIDTH).astype(F32), *tabs)


def _branch_kernel(sub, u_ref, ga_ref, gb_ref, gc_ref, oa_ref, ob_ref, oc_ref, wa_ref, wb_ref, wc_ref, o_ref):
    for m in range(u_ref.shape[0] // sub):
        rows = slice(m * sub, (m + 1) * sub)
        u = u_ref[rows, :]
        total = None
        for g_ref, o_in, w_ref in ((ga_ref, oa_ref, wa_ref), (gb_ref, ob_ref, wb_ref), (gc_ref, oc_ref, wc_ref)):
            gate = _sigmoid(jnp.dot(u, g_ref[...], preferred_element_type=F32))
            term = gate * jnp.dot(o_in[rows, :], w_ref[...], preferred_element_type=F32)
            total = term if total is None else total + term
        o_ref[rows, :] = total.astype(o_ref.dtype)


def _branch(u, w_in_pad, gate_cols, o_a, o_b, o_c, w_a, w_b, w_c, layer, tm=1024, tn=512, sub=PROJ_SUB):
    s, k = u.shape
    n = w_a.shape[2]
    act = lambda width: pl.BlockSpec((tm, width), lambda i, j: (i, 0))
    wgt = lambda width: pl.BlockSpec((None, width, tn), lambda i, j: (layer, 0, j))
    assert all(c % tn == 0 for c in gate_cols)
    gate = lambda col: pl.BlockSpec((None, k, tn), lambda i, j: (layer, 0, col // tn + j))
    return pl.pallas_call(
        functools.partial(_branch_kernel, sub),
        grid=(s // tm, n // tn),
        in_specs=[act(k)] + [gate(c) for c in gate_cols]
                 + [act(o_a.shape[1]), act(o_b.shape[1]), act(o_c.shape[1]),
                    wgt(w_a.shape[1]), wgt(w_b.shape[1]), wgt(w_c.shape[1])],
        out_specs=pl.BlockSpec((tm, tn), lambda i, j: (i, j)),
        out_shape=jax.ShapeDtypeStruct((s, n), BF16),
        compiler_params=_params("parallel", "arbitrary"),
        name="branch",
    )(u, w_in_pad, w_in_pad, w_in_pad, o_a, o_b, o_c, w_a, w_b, w_c)


def _outproj_kernel(m_ref, w_ref, x_ref, g_ref, x1_ref, u_ref):
    x1 = x_ref[...] + jnp.dot(m_ref[...], w_ref[...], preferred_element_type=F32)
    x1_ref[...] = x1
    ms = jnp.mean(x1 * x1, axis=-1, keepdims=True)
    u_ref[...] = (x1 * lax.rsqrt(ms + EPS) * g_ref[...]).astype(u_ref.dtype)


def _outproj(merged, w_out, layer, x, g, tm=512):
    s, d = x.shape
    return pl.pallas_call(
        _outproj_kernel,
        grid=(s // tm,),
        in_specs=[pl.BlockSpec((tm, d), lambda i: (i, 0)),
                  pl.BlockSpec((None, d, d), lambda i: (layer, 0, 0), pipeline_mode=pl.Buffered(1)),
                  pl.BlockSpec((tm, d), lambda i: (i, 0)),
                  pl.BlockSpec((1, d), lambda i: (0, 0))],
        out_specs=[pl.BlockSpec((tm, d), lambda i: (i, 0)),
                   pl.BlockSpec((tm, d), lambda i: (i, 0))],
        out_shape=[jax.ShapeDtypeStruct((s, d), F32), jax.ShapeDtypeStruct((s, d), BF16)],
        compiler_params=_params("parallel"),
        name="outproj",
    )(merged, w_out, x, g.reshape(1, d))


def _mlp_kernel(u_ref, wu_ref, wd_ref, x_ref, *rest):
    g_ref, o_ref, un_ref = rest if len(rest) == 3 else (None, rest[0], None)
    f = pl.program_id(1)

    @pl.when(f == 0)
    def _():
        o_ref[...] = x_ref[...]

    h = jnp.dot(u_ref[...], wu_ref[...], preferred_element_type=F32)
    h = jnp.square(jnp.maximum(h, 0.0)).astype(BF16)
    o_ref[...] += jnp.dot(h, wd_ref[...], preferred_element_type=F32)

    if un_ref is not None:
        @pl.when(f == pl.num_programs(1) - 1)
        def _():
            y = o_ref[...]
            ms = jnp.mean(y * y, axis=-1, keepdims=True)
            un_ref[...] = (y * lax.rsqrt(ms + EPS) * g_ref[...]).astype(un_ref.dtype)


def _mlp(u, w_up, w_down, layer, x, next_gain=None, tm=512, tf=1024):
    s, d = x.shape
    f = w_up.shape[2]
    row = pl.BlockSpec((tm, d), lambda i, j: (i, 0))
    args = [u, w_up, w_down, x]
    in_specs = [row,
                pl.BlockSpec((None, d, tf), lambda i, j: (layer, 0, j)),
                pl.BlockSpec((None, tf, d), lambda i, j: (layer, j, 0)),
                row]
    out_specs, out_shape = row, jax.ShapeDtypeStruct((s, d), F32)
    if next_gain is not None:
        args.append(next_gain.reshape(1, d))
        in_specs.append(pl.BlockSpec((1, d), lambda i, j: (0, 0)))
        out_specs, out_shape = [row, row], [out_shape, jax.ShapeDtypeStruct((s, d), BF16)]
    return pl.pallas_call(
        _mlp_kernel,
        grid=(s // tm, f // tf),
        in_specs=in_specs,
        out_specs=out_specs,
        out_shape=out_shape,
        compiler_params=_params("parallel", "arbitrary"),
        name="mlp",
    )(*args)


def _rope_tables(pos):
    half = ROPE_DIM // 2
    inv = ROPE_THETA ** (-jnp.arange(half, dtype=F32) / half)
    ang = pos.astype(F32)[:, None] * inv[None, :]
    cos, sin = jnp.cos(ang), jnp.sin(ang)
    n = pos.shape[0]
    pad = HEAD_DIM - ROPE_DIM
    t_cos = jnp.concatenate([cos, cos, jnp.ones((n, pad), F32)], axis=1)
    t_sin = jnp.concatenate([sin, sin, jnp.zeros((n, pad), F32)], axis=1)
    return tuple(jnp.tile(t, (1, LANES // HEAD_DIM)) for t in (t_cos, t_sin))


def _retention_rot_tables(pos):
    half = C_HEAD_DIM // 2
    inv = C_ROT_THETA ** (-jnp.arange(half, dtype=F32) / half)
    ang = pos.astype(F32)[:, None] * inv[None, :]
    cos, sin = jnp.cos(ang), jnp.sin(ang)
    return jnp.concatenate([cos, cos], axis=1), jnp.concatenate([-sin, sin], axis=1)


def _retention_decay_tables():
    c = RET_CHUNK
    log_g = jnp.log1p(-(2.0 ** (-5.0 - jnp.arange(C_HEADS, dtype=F32))))
    i = jnp.arange(c, dtype=F32)
    rel = i[:, None] - i[None, :]
    decay = jnp.where(rel >= 0, jnp.exp(log_g[:, None, None] * jnp.maximum(rel, 0.0)), 0.0)
    rows = lambda t: jnp.broadcast_to(t[:, :, None], (C_HEADS, c, C_HEAD_DIM))
    q_decay = rows(jnp.exp(log_g[:, None] * (i + 1.0)[None, :]))
    k_decay = rows(jnp.exp(log_g[:, None] * (c - 1 - i)[None, :]))
    chunk_decay = jnp.broadcast_to(jnp.exp(log_g * c)[:, None, None], (C_HEADS, C_HEAD_DIM, C_HEAD_DIM))
    return decay, q_decay, k_decay, chunk_decay


def _b_head_order():
    order = []
    for p in range(B_PAIRS):
        g2, a = divmod(p, B_PAIRS // 2)
        order += [B_REP * (2 * g2) + a, B_REP * (2 * g2 + 1) + a]
    return order


PREP_ROWS = 128


def _regroup_in_kernel(pieces, w_ref, *o_refs):
    for out, dst, src, width in pieces:
        o_refs[out][:, dst:dst + width] = w_ref[:, src:src + width].astype(BF16)


def _regroup_in(w_in, b_order):
    depth, k, _ = w_in.shape
    start = [sum(IN_SIZES[:i]) for i in range(len(IN_SIZES) + 1)]
    pieces = []
    for g in range(A_GROUPS):
        for t in range(3):
            pieces.append((0, (3 * g + t) * A_GROUP_WIDTH, start[t] + g * A_GROUP_WIDTH, A_GROUP_WIDTH))
    for slot, h in enumerate(b_order):
        pieces.append((1, slot * HEAD_DIM, start[3] + h * HEAD_DIM, HEAD_DIM))
    pieces.append((1, B_Q_WIDTH, start[4], 2 * B_KV_WIDTH))
    pieces.append((2, 0, start[6], start[13] - start[6]))
    widths = (3 * A_GROUPS * A_GROUP_WIDTH, B_Q_WIDTH + 2 * B_KV_WIDTH, start[13] - start[6])
    return pl.pallas_call(
        functools.partial(_regroup_in_kernel, tuple(pieces)),
        grid=(depth, k // PREP_ROWS),
        in_specs=[pl.BlockSpec((None, PREP_ROWS, start[13]), lambda l, i: (l, i, 0))],
        out_specs=[pl.BlockSpec((None, PREP_ROWS, w), lambda l, i: (l, i, 0)) for w in widths],
        out_shape=[jax.ShapeDtypeStruct((depth, k, w), BF16) for w in widths],
        compiler_params=_params("parallel", "parallel"),
        name="regroup_in",
    )(w_in)


def _permute_rows_kernel(order, w_ref, o_ref):
    for slot, h in enumerate(order):
        o_ref[slot * HEAD_DIM:(slot + 1) * HEAD_DIM, :] = w_ref[h * HEAD_DIM:(h + 1) * HEAD_DIM, :].astype(BF16)


def _permute_head_rows(w, order):
    depth, rows, n = w.shape
    return pl.pallas_call(
        functools.partial(_permute_rows_kernel, tuple(order)),
        grid=(depth,),
        in_specs=[pl.BlockSpec((None, rows, n), lambda l: (l, 0, 0))],
        out_specs=pl.BlockSpec((None, rows, n), lambda l: (l, 0, 0)),
        out_shape=jax.ShapeDtypeStruct((depth, rows, n), BF16),
        compiler_params=_params("parallel"),
        name="permute_rows",
    )(w)


def kernel(x, mix_norm, w_in, a_q_norm, a_k_norm, b_q_norm, b_k_norm, b_sinks, c_gn,
           w_br_a, w_br_b, w_br_c, w_out, mlp_norm, w_up, w_down):
    b, s, d = x.shape
    assert b == 1 and d == D_MODEL and s % A_SUPER == 0
    depth = w_in.shape[0]
    x = x.reshape(s, d)
    pos = jnp.arange(s)
    rope = _rope_tables(pos)
    rot_c = _retention_rot_tables(pos)
    decay_tabs = _retention_decay_tables()
    width2 = 2 * LANES
    bd = (jnp.arange(width2)[:, None] // HEAD_DIM == jnp.arange(width2)[None, :] // HEAD_DIM).astype(BF16)
    b_order = _b_head_order()
    q_scale = HEAD_DIM ** -0.5 * LOG2E
    tile = jnp.tile

    w_a, w_b, w_tail = _regroup_in(w_in, b_order)
    tail_col = lambda seg: sum(IN_SIZES[6:seg])
    wa = w_br_a.astype(BF16)
    wb = _permute_head_rows(w_br_b, b_order)
    wc = w_br_c.astype(BF16)
    wo = w_out.astype(BF16)
    wu = w_up.astype(BF16)
    wd = w_down.astype(BF16)
    a_width = 3 * A_GROUP_WIDTH
    b_width = B_Q_WIDTH + 2 * B_KV_WIDTH

    vec_a = jnp.concatenate([tile(a_q_norm * q_scale, (1, A_HEADS_PER_GROUP)),
                             tile(a_k_norm, (1, A_HEADS_PER_GROUP)),
                             jnp.ones((depth, A_GROUP_WIDTH), F32)], axis=1)
    vec_b = jnp.concatenate([tile(b_q_norm * q_scale, (1, B_Q_HEADS)), tile(b_k_norm, (1, B_KV_HEADS)),
                             jnp.ones((depth, B_KV_WIDTH), F32)], axis=1)
    vec_c = jnp.concatenate([jnp.ones((C_WIDTH,), F32), jnp.full((C_WIDTH,), C_HEAD_DIM ** -0.5, F32)])
    half = B_PAIRS // 2
    sinks = (b_sinks.astype(F32) * LOG2E).reshape(depth, 2, 2, half).transpose(0, 1, 3, 2).reshape(depth, B_Q_HEADS)
    assert np.arange(B_Q_HEADS).reshape(2, 2, half).transpose(0, 2, 1).reshape(-1).tolist() == b_order

    u = _rmsnorm(x, mix_norm[0])
    for l in range(depth):
        kinds = ["qk"] * (2 * A_PAIRS) + ["plain"] * A_PAIRS
        qkv = [_proj(u, w_a, l, g * a_width, a_width, kinds, tm=A_SUPER, tn=a_width,
                     vec=vec_a[l], tabs=rope, bd=bd, perm=dil) for g, (_, dil) in enumerate(A_DILATED)]
        o_a = _attn_a(qkv)

        kinds = ["qk"] * ((B_Q_WIDTH + B_KV_WIDTH) // LANES) + ["plain"] * (B_KV_WIDTH // LANES)
        bqkv = _proj(u, w_b, l, 0, b_width, kinds, tm=1024, tn=b_width, vec=vec_b[l], tabs=rope, bd=bd)
        o_b = _attn_b(bqkv, sinks[l])

        cqk = _proj(u, w_tail, l, tail_col(6), 2 * C_WIDTH, ["rot"] * 8, tm=2048, tn=1024, vec=vec_c, tabs=rot_c)
        kinds = ["plain"] * (C_WIDTH // LANES) + ["silu"] * (C_WIDTH // LANES)
        cvg = _proj(u, w_tail, l, tail_col(8), 2 * C_WIDTH, kinds, tm=1024, tn=2 * C_WIDTH)
        o_c = _retention(cqk, cvg, c_gn[l], decay_tabs)

        merged = _branch(u, w_tail, [tail_col(seg) for seg in (10, 11, 12)], o_a, o_b, o_c, wa, wb, wc, l)
        x, u2 = _outproj(merged, wo, l, x, mlp_norm[l])
        if l + 1 < depth:
            x, u = _mlp(u2, wu, wd, l, x, next_gain=mix_norm[l + 1])
        else:
            x = _mlp(u2, wu, wd, l, x)
    return x.reshape(b, s, d)
```

```python
import functools
import math

import jax
import jax.numpy as jnp
import numpy as np
from jax import lax
from jax.experimental import pallas as pl
from jax.experimental.pallas import tpu as pltpu

F32 = jnp.float32
BF16 = jnp.bfloat16

D_MODEL = 2048
EPS = 1e-6
NEG_INF = -1e30
BLOCK = 128
LANES = 128
HEAD_DIM = 64
ROPE_DIM = HEAD_DIM // 4
ROPE_THETA = 500000.0
A_DILATED = ((128, 1), (512, 4), (2048, 16))
A_GROUPS = len(A_DILATED)
A_HEADS_PER_GROUP = 6
A_GROUP_WIDTH = A_HEADS_PER_GROUP * HEAD_DIM
A_PAIRS = A_GROUP_WIDTH // LANES
A_SUPER = BLOCK * max(d for _, d in A_DILATED)
B_WINDOW = 128
B_Q_HEADS = 16
B_KV_HEADS = 4
B_REP = B_Q_HEADS // B_KV_HEADS
B_Q_WIDTH = B_Q_HEADS * HEAD_DIM
B_KV_WIDTH = B_KV_HEADS * HEAD_DIM
B_PAIRS = B_Q_WIDTH // LANES
C_HEADS = 8
C_HEAD_DIM = 128
C_WIDTH = C_HEADS * C_HEAD_DIM
C_ROT_THETA = 10000.0
RET_CHUNK = 256
RET_UNROLL = 8
D_FF = 4 * D_MODEL
IN_SIZES = (A_GROUPS * A_GROUP_WIDTH,) * 3 + (B_Q_WIDTH, B_KV_WIDTH, B_KV_WIDTH) + (C_WIDTH,) * 4 + (D_MODEL,) * 3

VMEM_LIMIT = 56 * 1024 * 1024
LOOP_UNROLL = 16
LOG2E = math.log2(math.e)
PROJ_SUB = 512


def _params(*sem):
    return pltpu.CompilerParams(dimension_semantics=sem, vmem_limit_bytes=VMEM_LIMIT)


def _sigmoid(a):
    return 1.0 / (1.0 + jnp.exp(-a))


def _rmsnorm_kernel(x_ref, g_ref, o_ref):
    x = x_ref[...]
    ms = jnp.mean(x * x, axis=-1, keepdims=True)
    o_ref[...] = (x * lax.rsqrt(ms + EPS) * g_ref[...]).astype(o_ref.dtype)


def _rmsnorm(x, g, tm=512):
    s, d = x.shape
    return pl.pallas_call(
        _rmsnorm_kernel,
        grid=(s // tm,),
        in_specs=[pl.BlockSpec((tm, d), lambda i: (i, 0)),
                  pl.BlockSpec((1, d), lambda i: (0, 0))],
        out_specs=pl.BlockSpec((tm, d), lambda i: (i, 0)),
        out_shape=jax.ShapeDtypeStruct((s, d), BF16),
        compiler_params=_params("parallel"),
        name="rmsnorm",
    )(x, g.reshape(1, d))


PERM_STAGE = 4


def _store_residue_major(o_ref, scr_a, scr_b, y, c, sl, row0, perm):
    sub = y.shape[0]
    chunk_rows = BLOCK * perm
    base = row0 // chunk_rows * chunk_rows + row0 % chunk_rows // perm
    n = sub // perm
    scr_a[c] = y
    if perm <= PERM_STAGE:
        for r in range(perm):
            o_ref[base + r * BLOCK:base + r * BLOCK + n, sl] = (
                scr_a[c, pl.ds(r, n, stride=perm), :].astype(o_ref.dtype))
        return
    outer = perm // PERM_STAGE
    assert outer <= PERM_STAGE and perm % PERM_STAGE == 0
    part = sub // PERM_STAGE
    for r0 in range(PERM_STAGE):
        scr_b[c, r0 * part:(r0 + 1) * part, :] = scr_a[c, pl.ds(r0, part, stride=PERM_STAGE), :]
    for r0 in range(PERM_STAGE):
        for r1 in range(outer):
            r = r1 * PERM_STAGE + r0
            o_ref[base + r * BLOCK:base + r * BLOCK + n, sl] = (
                scr_b[c, pl.ds(r0 * part + r1, n, stride=outer), :].astype(o_ref.dtype))


def _proj_kernel(tiles, sub, *refs):
    refs = list(refs)
    any_perm = any(p > 1 for _, p in tiles)
    scr_b = refs.pop() if any_perm else None
    scr_a = refs.pop() if any_perm else None
    o_ref = refs.pop()
    u_ref, w_ref = refs[0], refs[1]
    rest = refs[2:]
    all_kinds = [k for kinds, _ in tiles for k in kinds]
    need_tab = any(k in ("qk", "rot") for k in all_kinds)
    vec_ref = rest.pop(0) if need_tab else None
    cos_ref, sin_ref = (rest.pop(0), rest.pop(0)) if need_tab else (None, None)
    bd_ref = rest.pop(0) if "qk" in all_kinds else None
    dim = lax.broadcasted_iota(jnp.int32, (sub, LANES), 1) & (HEAD_DIM - 1)
    half = ROPE_DIM // 2

    def tile_body(kinds, perm):
        n_qk = sum(k == "qk" for k in kinds)
        assert n_qk % 2 == 0 and all(k == "qk" for k in kinds[:n_qk])
        for m in range(u_ref.shape[0] // sub):
            rows = slice(m * sub, (m + 1) * sub)
            acc = jnp.dot(u_ref[rows, :], w_ref[...], preferred_element_type=F32)
            ss = []
            for c in range(n_qk // 2):
                a2 = acc[:, 2 * c * LANES:2 * (c + 1) * LANES]
                s2 = jnp.dot((a2 * a2).astype(BF16), bd_ref[...], preferred_element_type=F32)
                ss += [s2[:, :LANES], s2[:, LANES:]]
            for c, kind in enumerate(kinds):
                sl = slice(c * LANES, (c + 1) * LANES)
                a = acc[:, sl]
                if kind == "qk":
                    y = a * lax.rsqrt(ss[c] * (1.0 / HEAD_DIM) + EPS) * vec_ref[:, sl]
                    turned = jnp.where(dim < half, -pltpu.roll(y, LANES - half, 1),
                                       jnp.where(dim < ROPE_DIM, pltpu.roll(y, half, 1), 0.0))
                    y = y * cos_ref[rows, :] + turned * sin_ref[rows, :]
                elif kind == "rot":
                    y = (a * cos_ref[rows, :] + pltpu.roll(a, LANES // 2, 1) * sin_ref[rows, :]) * vec_ref[:, sl]
                elif kind == "sigmoid":
                    y = _sigmoid(a)
                elif kind == "silu":
                    y = a * _sigmoid(a)
                elif kind == "relu2":
                    y = jnp.square(jnp.maximum(a, 0.0))
                else:
                    y = a
                if perm == 1:
                    o_ref[rows, sl] = y.astype(o_ref.dtype)
                else:
                    _store_residue_major(o_ref, scr_a, scr_b, y, c, sl, m * sub, perm)

    if len(tiles) == 1:
        tile_body(*tiles[0])
    else:
        for j, (kinds, perm) in enumerate(tiles):
            pl.when(pl.program_id(1) == j)(functools.partial(tile_body, kinds, perm))


def _proj(u, w, layer, col0, tiles, tm, tn, vec=None, tabs=None, bd=None, sub=PROJ_SUB):
    s, k = u.shape
    tiles = tuple((tuple(kinds), perm) for kinds, perm in tiles)
    n = len(tiles) * tn
    assert col0 % tn == 0 and s % tm == 0 and tm % sub == 0
    for kinds, perm in tiles:
        assert len(kinds) == tn // LANES
        assert perm == 1 or (sub % perm == 0 and (BLOCK * perm) % sub == 0 and tm % (BLOCK * perm) == 0)
    if all(t == tiles[0] for t in tiles):
        tiles = tiles[:1]
    j0 = col0 // tn
    args = [u, w]
    in_specs = [pl.BlockSpec((tm, k), lambda i, j: (i, 0)),
                pl.BlockSpec((None, k, tn), lambda i, j: (layer, 0, j0 + j))]
    if tabs is not None:
        args.append(vec.reshape(1, n).astype(F32))
        in_specs.append(pl.BlockSpec((1, tn), lambda i, j: (0, j)))
        for t in tabs:
            args.append(t)
            in_specs.append(pl.BlockSpec((tm, LANES), lambda i, j: (i, 0)))
    if bd is not None:
        args.append(bd)
        in_specs.append(pl.BlockSpec(bd.shape, lambda i, j: (0, 0)))
    scratch = []
    if any(p > 1 for _, p in tiles):
        scratch = [pltpu.VMEM((tn // LANES, sub, LANES), F32)] * 2
    return pl.pallas_call(
        functools.partial(_proj_kernel, tiles, sub),
        grid=(s // tm, n // tn),
        in_specs=in_specs,
        out_specs=pl.BlockSpec((tm, tn), lambda i, j: (i, j)),
        out_shape=jax.ShapeDtypeStruct((s, n), BF16),
        scratch_shapes=scratch,
        compiler_params=_params("parallel", "arbitrary"),
        name="proj_" + tiles[0][0][0],
    )(*args)


def _half_masks(rows):
    lane = lax.broadcasted_iota(jnp.int32, (rows, LANES), 1)
    lo = jnp.where(lane < HEAD_DIM, 1.0, 0.0)
    return lo.astype(BF16), (1.0 - lo).astype(BF16)


def _pair_attend(q, k2, v2, bias, sink_key=False):
    if sink_key:
        rows = 16
        keep = jnp.where(lax.broadcasted_iota(jnp.int32, (rows, LANES), 0) > 0, 1.0, 0.0).astype(BF16)
        k2 = jnp.concatenate([k2[:rows] * keep, k2[rows:]], axis=0)
        v2 = jnp.concatenate([v2[:rows] * keep, v2[rows:]], axis=0)
    lo_b, hi_b = _half_masks(BLOCK)
    q2 = jnp.concatenate([q * lo_b, q * hi_b], axis=0)
    s2 = lax.dot_general(q2, k2, (((1,), (1,)), ((), ())), preferred_element_type=F32) + bias
    m2 = jnp.max(s2, axis=-1, keepdims=True)
    p2 = jnp.exp2(s2 - m2).astype(BF16)
    vx = jnp.concatenate([v2, jnp.ones_like(v2)], axis=1)
    r2 = jnp.dot(p2, vx, preferred_element_type=F32)
    lo = lax.broadcasted_iota(jnp.int32, (BLOCK, LANES), 1) < HEAD_DIM
    num = jnp.where(lo, r2[:BLOCK, :LANES], r2[BLOCK:, :LANES])
    den = jnp.where(lo, r2[:BLOCK, LANES:], r2[BLOCK:, LANES:])
    mx = jnp.where(lo, m2[:BLOCK], m2[BLOCK:])
    return num, den, mx


def _fill_bias(bias_ref, max_dist, sinks=None):
    row = lax.broadcasted_iota(jnp.int32, (2 * BLOCK, 2 * BLOCK), 0)
    kj = lax.broadcasted_iota(jnp.int32, (2 * BLOCK, 2 * BLOCK), 1)
    dist = (row & (BLOCK - 1)) + BLOCK - kj
    band = jnp.where(dist >= 0, jnp.where(dist <= max_dist, 0.0, NEG_INF), NEG_INF)
    first = jnp.where(kj >= BLOCK, band, NEG_INF)
    if sinks is not None:
        assert max_dist < BLOCK
        sink = jnp.where(row < BLOCK, sinks[0], sinks[1])
        band = jnp.where(kj == 0, sink, band)
        first = jnp.where(kj == 0, sink, first)
    bias_ref[0] = band
    bias_ref[1] = first


def _fill_ext(ext_ref, prev_ref, cur_ref):
    p = prev_ref.shape[0]
    ext_ref[0:p, :] = prev_ref[...]
    ext_ref[p:, :] = cur_ref[...]


def _attn_a_kernel(*refs):
    ins, (o_ref,), scr = refs[:15], refs[15:16], refs[16:]
    sb = pl.program_id(0)
    o_scr, l_scr, bias_ref = scr[0], scr[1], scr[2]
    _fill_bias(bias_ref, BLOCK)
    for g, (_, dil) in enumerate(A_DILATED):
        q_ref, k_ref, v_ref, kp_ref, vp_ref = ins[5 * g:5 * g + 5]
        kx_ref, vx_ref = scr[3 + 2 * g], scr[4 + 2 * g]
        _fill_ext(kx_ref, kp_ref, k_ref)
        _fill_ext(vx_ref, vp_ref, v_ref)
        prev_rows = BLOCK * dil

        def body(b, carry, g=g, dil=dil, q_ref=q_ref, kx_ref=kx_ref, vx_ref=vx_ref, prev_rows=prev_rows):
            row = pl.multiple_of(b * BLOCK, BLOCK)
            q = q_ref[pl.ds(row, BLOCK), :]
            k2 = jnp.concatenate([kx_ref[pl.ds(row, BLOCK), :],
                                  kx_ref[pl.ds(row + prev_rows, BLOCK), :]], axis=0)
            v2 = jnp.concatenate([vx_ref[pl.ds(row, BLOCK), :],
                                  vx_ref[pl.ds(row + prev_rows, BLOCK), :]], axis=0)
            no_prev = jnp.logical_and(sb == 0, b < dil)
            num, den, mx = _pair_attend(q, k2, v2, bias_ref[no_prev.astype(jnp.int32)])
            if dil == 1:
                dst = pl.ds(row, BLOCK)
            else:
                chunk = b // dil
                res = b - chunk * dil
                dst = pl.ds(chunk * prev_rows + res, BLOCK, stride=dil)
            o_scr[g, dst, :] = num / den
            l_scr[g, dst, :] = mx + jnp.log2(den)
            return carry

        lax.fori_loop(0, A_SUPER // BLOCK, body, 0, unroll=LOOP_UNROLL)

    l0, l1, l2 = l_scr[0], l_scr[1], l_scr[2]
    lm = jnp.maximum(jnp.maximum(l0, l1), l2)
    w0, w1, w2 = jnp.exp2(l0 - lm), jnp.exp2(l1 - lm), jnp.exp2(l2 - lm)
    o = (w0 * o_scr[0] + w1 * o_scr[1] + w2 * o_scr[2]) / (w0 + w1 + w2)
    o_ref[...] = o.astype(o_ref.dtype)


def _attn_a(qkv):
    s = qkv.shape[0]
    args, in_specs = [], []
    scratch = [pltpu.VMEM((A_GROUPS, A_SUPER, LANES), F32),
               pltpu.VMEM((A_GROUPS, A_SUPER, LANES), F32),
               pltpu.VMEM((2, 2 * BLOCK, 2 * BLOCK), F32)]
    for g, (_, dil) in enumerate(A_DILATED):
        prev_rows = BLOCK * dil
        per = A_SUPER // prev_rows
        col = 3 * A_PAIRS * g

        def cur_map(off):
            return lambda i, p: (i, off + p)

        def prev_map(off, per=per):
            return lambda i, p: (jnp.maximum(i * per - 1, 0), off + p)

        args += [qkv] * 5
        in_specs += [pl.BlockSpec((A_SUPER, LANES), cur_map(col)),
                     pl.BlockSpec((A_SUPER, LANES), cur_map(col + A_PAIRS)),
                     pl.BlockSpec((A_SUPER, LANES), cur_map(col + 2 * A_PAIRS)),
                     pl.BlockSpec((prev_rows, LANES), prev_map(col + A_PAIRS)),
                     pl.BlockSpec((prev_rows, LANES), prev_map(col + 2 * A_PAIRS))]
        scratch += [pltpu.VMEM((prev_rows + A_SUPER, LANES), BF16)] * 2
    return pl.pallas_call(
        _attn_a_kernel,
        grid=(s // A_SUPER, A_PAIRS),
        in_specs=in_specs,
        out_specs=pl.BlockSpec((A_SUPER, LANES), lambda i, p: (i, p)),
        out_shape=jax.ShapeDtypeStruct((s, A_GROUP_WIDTH), BF16),
        scratch_shapes=scratch,
        compiler_params=_params("parallel", "parallel"),
        name="attn_a",
    )(*args)


def _attn_b_kernel(q_ref, k_ref, v_ref, kp_ref, vp_ref, sink_ref, o_ref, kx_ref, vx_ref, bias_ref):
    i = pl.program_id(0)
    p = pl.program_id(1)
    _fill_bias(bias_ref, B_WINDOW - 1, sinks=(sink_ref[2 * p], sink_ref[2 * p + 1]))
    _fill_ext(kx_ref, kp_ref, k_ref)
    _fill_ext(vx_ref, vp_ref, v_ref)

    def body(b, carry):
        row = pl.multiple_of(b * BLOCK, BLOCK)
        q = q_ref[pl.ds(row, BLOCK), :]
        k2 = kx_ref[pl.ds(row, 2 * BLOCK), :]
        v2 = vx_ref[pl.ds(row, 2 * BLOCK), :]
        no_prev = jnp.logical_and(i == 0, b == 0)
        num, den, _ = _pair_attend(q, k2, v2, bias_ref[no_prev.astype(jnp.int32)], sink_key=True)
        o_ref[pl.ds(row, BLOCK), :] = (num / den).astype(o_ref.dtype)
        return carry

    lax.fori_loop(0, q_ref.shape[0] // BLOCK, body, 0, unroll=LOOP_UNROLL)


def _attn_b(qkv, sinks, tq=2048):
    s = qkv.shape[0]
    per = tq // BLOCK
    kv = lambda p: p // (B_PAIRS // 2)
    k0 = B_PAIRS
    v0 = B_PAIRS + B_KV_WIDTH // LANES
    return pl.pallas_call(
        _attn_b_kernel,
        grid=(s // tq, B_PAIRS),
        in_specs=[pl.BlockSpec((tq, LANES), lambda i, p: (i, p)),
                  pl.BlockSpec((tq, LANES), lambda i, p: (i, k0 + kv(p))),
                  pl.BlockSpec((tq, LANES), lambda i, p: (i, v0 + kv(p))),
                  pl.BlockSpec((BLOCK, LANES), lambda i, p: (jnp.maximum(i * per - 1, 0), k0 + kv(p))),
                  pl.BlockSpec((BLOCK, LANES), lambda i, p: (jnp.maximum(i * per - 1, 0), v0 + kv(p))),
                  pl.BlockSpec(memory_space=pltpu.SMEM)],
        out_specs=pl.BlockSpec((tq, LANES), lambda i, p: (i, p)),
        out_shape=jax.ShapeDtypeStruct((s, B_Q_WIDTH), BF16),
        scratch_shapes=[pltpu.VMEM((BLOCK + tq, LANES), BF16)] * 2
                       + [pltpu.VMEM((2, 2 * BLOCK, 2 * BLOCK), F32)],
        compiler_params=_params("parallel", "parallel"),
        name="attn_b",
    )(qkv, qkv, qkv, qkv, qkv, sinks)


def _retention_kernel(q_ref, k_ref, v_ref, g_ref, gn_ref, decay_ref, qd_ref, kd_ref, cd_ref, o_ref, state_ref):
    i = pl.program_id(0)
    h = pl.program_id(1)

    @pl.when(i == 0)
    def _():
        state_ref[h] = jnp.zeros((C_HEAD_DIM, C_HEAD_DIM), F32)

    gn = gn_ref[...]

    def body(n, state):
        row = pl.multiple_of(n * RET_CHUNK, RET_CHUNK)
        q = q_ref[pl.ds(row, RET_CHUNK), :]
        k = k_ref[pl.ds(row, RET_CHUNK), :]
        v = v_ref[pl.ds(row, RET_CHUNK), :]
        s = lax.dot_general(q, k, (((1,), (1,)), ((), ())), preferred_element_type=F32) * decay_ref[0]
        inner = jnp.dot(s.astype(BF16), v, preferred_element_type=F32)
        cross = jnp.dot(q, state.astype(BF16), preferred_element_type=F32) * qd_ref[0]
        kd_t = (k.astype(F32) * kd_ref[0]).T.astype(BF16)
        new_state = cd_ref[0] * state + jnp.dot(kd_t, v, preferred_element_type=F32)
        y = inner + cross
        y = y * lax.rsqrt(jnp.mean(y * y, axis=-1, keepdims=True) + EPS)
        o_ref[pl.ds(row, RET_CHUNK), :] = (g_ref[pl.ds(row, RET_CHUNK), :].astype(F32) * (y * gn)).astype(o_ref.dtype)
        return new_state

    state_ref[h] = lax.fori_loop(0, q_ref.shape[0] // RET_CHUNK, body, state_ref[h], unroll=RET_UNROLL)


def _retention(c_all, c_gn, tabs, tq=2048):
    s = c_all.shape[0]
    v_off = 2 * C_HEADS
    g_off = 3 * C_HEADS
    tab = lambda t: pl.BlockSpec((1,) + t.shape[1:], lambda i, h: (h, 0, 0))
    return pl.pallas_call(
        _retention_kernel,
        grid=(s // tq, C_HEADS),
        in_specs=[pl.BlockSpec((tq, LANES), lambda i, h: (i, h)),
                  pl.BlockSpec((tq, LANES), lambda i, h: (i, C_HEADS + h)),
                  pl.BlockSpec((tq, LANES), lambda i, h: (i, v_off + h)),
                  pl.BlockSpec((tq, LANES), lambda i, h: (i, g_off + h)),
                  pl.BlockSpec((1, LANES), lambda i, h: (0, h))] + [tab(t) for t in tabs],
        out_specs=pl.BlockSpec((tq, LANES), lambda i, h: (i, h)),
        out_shape=jax.ShapeDtypeStruct((s, C_WIDTH), BF16),
        scratch_shapes=[pltpu.VMEM((C_HEADS, C_HEAD_DIM, C_HEAD_DIM), F32)],
        compiler_params=_params("arbitrary", "arbitrary"),
        name="retention",
    )(c_all, c_all, c_all, c_all, c_gn.reshape(1, C_WIDTH).astype(F32), *tabs)


def _branch_kernel(sub, u_ref, ga_ref, gb_ref, gc_ref, oa_ref, ob_ref, oc_ref, wa_ref, wb_ref, wc_ref, o_ref):
    for m in range(u_ref.shape[0] // sub):
        rows = slice(m * sub, (m + 1) * sub)
        u = u_ref[rows, :]
        total = None
        for g_ref, o_in, w_ref in ((ga_ref, oa_ref, wa_ref), (gb_ref, ob_ref, wb_ref), (gc_ref, oc_ref, wc_ref)):
            gate = _sigmoid(jnp.dot(u, g_ref[...], preferred_element_type=F32))
            term = gate * jnp.dot(o_in[rows, :], w_ref[...], preferred_element_type=F32)
            total = term if total is None else total + term
        o_ref[rows, :] = total.astype(o_ref.dtype)


def _branch(u, w_in_pad, gate_cols, o_a, o_b, o_c, w_a, w_b, w_c, layer, tm=1024, tn=512, sub=PROJ_SUB):
    s, k = u.shape
    n = w_a.shape[2]
    act = lambda width: pl.BlockSpec((tm, width), lambda i, j: (i, 0))
    wgt = lambda width: pl.BlockSpec((None, width, tn), lambda i, j: (layer, 0, j))
    assert all(c % tn == 0 for c in gate_cols)
    gate = lambda col: pl.BlockSpec((None, k, tn), lambda i, j: (layer, 0, col // tn + j))
    return pl.pallas_call(
        functools.partial(_branch_kernel, sub),
        grid=(s // tm, n // tn),
        in_specs=[act(k)] + [gate(c) for c in gate_cols]
                 + [act(o_a.shape[1]), act(o_b.shape[1]), act(o_c.shape[1]),
                    wgt(w_a.shape[1]), wgt(w_b.shape[1]), wgt(w_c.shape[1])],
        out_specs=pl.BlockSpec((tm, tn), lambda i, j: (i, j)),
        out_shape=jax.ShapeDtypeStruct((s, n), BF16),
        compiler_params=_params("parallel", "arbitrary"),
        name="branch",
    )(u, w_in_pad, w_in_pad, w_in_pad, o_a, o_b, o_c, w_a, w_b, w_c)


def _outproj_kernel(m_ref, w_ref, x_ref, g_ref, x1_ref, u_ref):
    x1 = x_ref[...] + jnp.dot(m_ref[...], w_ref[...], preferred_element_type=F32)
    x1_ref[...] = x1
    ms = jnp.mean(x1 * x1, axis=-1, keepdims=True)
    u_ref[...] = (x1 * lax.rsqrt(ms + EPS) * g_ref[...]).astype(u_ref.dtype)


def _outproj(merged, w_out, layer, x, g, tm=512):
    s, d = x.shape
    return pl.pallas_call(
        _outproj_kernel,
        grid=(s // tm,),
        in_specs=[pl.BlockSpec((tm, d), lambda i: (i, 0)),
                  pl.BlockSpec((None, d, d), lambda i: (layer, 0, 0), pipeline_mode=pl.Buffered(1)),
                  pl.BlockSpec((tm, d), lambda i: (i, 0)),
                  pl.BlockSpec((1, d), lambda i: (0, 0))],
        out_specs=[pl.BlockSpec((tm, d), lambda i: (i, 0)),
                   pl.BlockSpec((tm, d), lambda i: (i, 0))],
        out_shape=[jax.ShapeDtypeStruct((s, d), F32), jax.ShapeDtypeStruct((s, d), BF16)],
        compiler_params=_params("parallel"),
        name="outproj",
    )(merged, w_out, x, g.reshape(1, d))


def _mlp_kernel(u_ref, wu_ref, wd_ref, x_ref, *rest):
    g_ref, o_ref, un_ref = rest if len(rest) == 3 else (None, rest[0], None)
    f = pl.program_id(1)

    @pl.when(f == 0)
    def _():
        o_ref[...] = x_ref[...]

    h = jnp.dot(u_ref[...], wu_ref[...], preferred_element_type=F32)
    h = jnp.square(jnp.maximum(h, 0.0)).astype(BF16)
    o_ref[...] += jnp.dot(h, wd_ref[...], preferred_element_type=F32)

    if un_ref is not None:
        @pl.when(f == pl.num_programs(1) - 1)
        def _():
            y = o_ref[...]
            ms = jnp.mean(y * y, axis=-1, keepdims=True)
            un_ref[...] = (y * lax.rsqrt(ms + EPS) * g_ref[...]).astype(un_ref.dtype)


def _mlp(u, w_up, w_down, layer, x, next_gain=None, tm=512, tf=1024):
    s, d = x.shape
    f = w_up.shape[2]
    row = pl.BlockSpec((tm, d), lambda i, j: (i, 0))
    args = [u, w_up, w_down, x]
    in_specs = [row,
                pl.BlockSpec((None, d, tf), lambda i, j: (layer, 0, j)),
                pl.BlockSpec((None, tf, d), lambda i, j: (layer, j, 0)),
                row]
    out_specs, out_shape = row, jax.ShapeDtypeStruct((s, d), F32)
    if next_gain is not None:
        args.append(next_gain.reshape(1, d))
        in_specs.append(pl.BlockSpec((1, d), lambda i, j: (0, 0)))
        out_specs, out_shape = [row, row], [out_shape, jax.ShapeDtypeStruct((s, d), BF16)]
    return pl.pallas_call(
        _mlp_kernel,
        grid=(s // tm, f // tf),
        in_specs=in_specs,
        out_specs=out_specs,
        out_shape=out_shape,
        compiler_params=_params("parallel", "arbitrary"),
        name="mlp",
    )(*args)


def _rope_tables(pos):
    half = ROPE_DIM // 2
    inv = ROPE_THETA ** (-jnp.arange(half, dtype=F32) / half)
    ang = pos.astype(F32)[:, None] * inv[None, :]
    cos, sin = jnp.cos(ang), jnp.sin(ang)
    n = pos.shape[0]
    pad = HEAD_DIM - ROPE_DIM
    t_cos = jnp.concatenate([cos, cos, jnp.ones((n, pad), F32)], axis=1)
    t_sin = jnp.concatenate([sin, sin, jnp.zeros((n, pad), F32)], axis=1)
    return tuple(jnp.tile(t, (1, LANES // HEAD_DIM)) for t in (t_cos, t_sin))


def _retention_rot_tables(pos):
    half = C_HEAD_DIM // 2
    inv = C_ROT_THETA ** (-jnp.arange(half, dtype=F32) / half)
    ang = pos.astype(F32)[:, None] * inv[None, :]
    cos, sin = jnp.cos(ang), jnp.sin(ang)
    return jnp.concatenate([cos, cos], axis=1), jnp.concatenate([-sin, sin], axis=1)


def _retention_decay_tables():
    c = RET_CHUNK
    log_g = jnp.log1p(-(2.0 ** (-5.0 - jnp.arange(C_HEADS, dtype=F32))))
    i = jnp.arange(c, dtype=F32)
    rel = i[:, None] - i[None, :]
    decay = jnp.where(rel >= 0, jnp.exp(log_g[:, None, None] * jnp.maximum(rel, 0.0)), 0.0)
    rows = lambda t: jnp.broadcast_to(t[:, :, None], (C_HEADS, c, C_HEAD_DIM))
    q_decay = rows(jnp.exp(log_g[:, None] * (i + 1.0)[None, :]))
    k_decay = rows(jnp.exp(log_g[:, None] * (c - 1 - i)[None, :]))
    chunk_decay = jnp.broadcast_to(jnp.exp(log_g * c)[:, None, None], (C_HEADS, C_HEAD_DIM, C_HEAD_DIM))
    return decay, q_decay, k_decay, chunk_decay


def _b_head_order():
    order = []
    for p in range(B_PAIRS):
        g2, a = divmod(p, B_PAIRS // 2)
        order += [B_REP * (2 * g2) + a, B_REP * (2 * g2 + 1) + a]
    return order


PREP_ROWS = 128


def _regroup_in_kernel(pieces, w_ref, *o_refs):
    for out, dst, src, width in pieces:
        o_refs[out][:, dst:dst + width] = w_ref[:, src:src + width].astype(BF16)


def _regroup_in(w_in, b_order):
    depth, k, _ = w_in.shape
    start = [sum(IN_SIZES[:i]) for i in range(len(IN_SIZES) + 1)]
    pieces = []
    for g in range(A_GROUPS):
        for t in range(3):
            pieces.append((0, (3 * g + t) * A_GROUP_WIDTH, start[t] + g * A_GROUP_WIDTH, A_GROUP_WIDTH))
    for slot, h in enumerate(b_order):
        pieces.append((1, slot * HEAD_DIM, start[3] + h * HEAD_DIM, HEAD_DIM))
    pieces.append((1, B_Q_WIDTH, start[4], 2 * B_KV_WIDTH))
    pieces.append((2, 0, start[6], start[13] - start[6]))
    widths = (3 * A_GROUPS * A_GROUP_WIDTH, B_Q_WIDTH + 2 * B_KV_WIDTH, start[13] - start[6])
    return pl.pallas_call(
        functools.partial(_regroup_in_kernel, tuple(pieces)),
        grid=(depth, k // PREP_ROWS),
        in_specs=[pl.BlockSpec((None, PREP_ROWS, start[13]), lambda l, i: (l, i, 0))],
        out_specs=[pl.BlockSpec((None, PREP_ROWS, w), lambda l, i: (l, i, 0)) for w in widths],
        out_shape=[jax.ShapeDtypeStruct((depth, k, w), BF16) for w in widths],
        compiler_params=_params("parallel", "parallel"),
        name="regroup_in",
    )(w_in)


def _permute_rows_kernel(order, w_ref, o_ref):
    for slot, h in enumerate(order):
        o_ref[slot * HEAD_DIM:(slot + 1) * HEAD_DIM, :] = w_ref[h * HEAD_DIM:(h + 1) * HEAD_DIM, :].astype(BF16)


def _permute_head_rows(w, order):
    depth, rows, n = w.shape
    return pl.pallas_call(
        functools.partial(_permute_rows_kernel, tuple(order)),
        grid=(depth,),
        in_specs=[pl.BlockSpec((None, rows, n), lambda l: (l, 0, 0))],
        out_specs=pl.BlockSpec((None, rows, n), lambda l: (l, 0, 0)),
        out_shape=jax.ShapeDtypeStruct((depth, rows, n), BF16),
        compiler_params=_params("parallel"),
        name="permute_rows",
    )(w)


def kernel(x, mix_norm, w_in, a_q_norm, a_k_norm, b_q_norm, b_k_norm, b_sinks, c_gn,
           w_br_a, w_br_b, w_br_c, w_out, mlp_norm, w_up, w_down):
    b, s, d = x.shape
    assert b == 1 and d == D_MODEL and s % A_SUPER == 0
    depth = w_in.shape[0]
    x = x.reshape(s, d)
    pos = jnp.arange(s)
    rope = _rope_tables(pos)
    rot_c = _retention_rot_tables(pos)
    decay_tabs = _retention_decay_tables()
    width2 = 2 * LANES
    bd = (jnp.arange(width2)[:, None] // HEAD_DIM == jnp.arange(width2)[None, :] // HEAD_DIM).astype(BF16)
    b_order = _b_head_order()
    q_scale = HEAD_DIM ** -0.5 * LOG2E
    tile = jnp.tile

    w_a, w_b, w_tail = _regroup_in(w_in, b_order)
    tail_col = lambda seg: sum(IN_SIZES[6:seg])
    wa = w_br_a.astype(BF16)
    wb = _permute_head_rows(w_br_b, b_order)
    wc = w_br_c.astype(BF16)
    wo = w_out.astype(BF16)
    wu = w_up.astype(BF16)
    wd = w_down.astype(BF16)
    a_width = 3 * A_GROUP_WIDTH
    b_width = B_Q_WIDTH + 2 * B_KV_WIDTH

    vec_a = tile(jnp.concatenate([tile(a_q_norm * q_scale, (1, A_HEADS_PER_GROUP)),
                                  tile(a_k_norm, (1, A_HEADS_PER_GROUP)),
                                  jnp.ones((depth, A_GROUP_WIDTH), F32)], axis=1), (1, A_GROUPS))
    vec_b = jnp.concatenate([tile(b_q_norm * q_scale, (1, B_Q_HEADS)), tile(b_k_norm, (1, B_KV_HEADS)),
                             jnp.ones((depth, B_KV_WIDTH), F32)], axis=1)
    vec_c = jnp.concatenate([jnp.ones((C_WIDTH,), F32), jnp.full((C_WIDTH,), C_HEAD_DIM ** -0.5, F32),
                             jnp.ones((2 * C_WIDTH,), F32)])
    half = B_PAIRS // 2
    sinks = (b_sinks.astype(F32) * LOG2E).reshape(depth, 2, 2, half).transpose(0, 1, 3, 2).reshape(depth, B_Q_HEADS)
    assert np.arange(B_Q_HEADS).reshape(2, 2, half).transpose(0, 2, 1).reshape(-1).tolist() == b_order

    u = _rmsnorm(x, mix_norm[0])
    for l in range(depth):
        kinds = ["qk"] * (2 * A_PAIRS) + ["plain"] * A_PAIRS
        qkv = _proj(u, w_a, l, 0, [(kinds, dil) for _, dil in A_DILATED], tm=A_SUPER, tn=a_width,
                    vec=vec_a[l], tabs=rope, bd=bd)
        o_a = _attn_a(qkv)

        kinds = ["qk"] * ((B_Q_WIDTH + B_KV_WIDTH) // LANES) + ["plain"] * (B_KV_WIDTH // LANES)
        bqkv = _proj(u, w_b, l, 0, [(kinds, 1)], tm=1024, tn=b_width, vec=vec_b[l], tabs=rope, bd=bd)
        o_b = _attn_b(bqkv, sinks[l])

        heads = C_WIDTH // LANES
        c_all = _proj(u, w_tail, l, tail_col(6), [(["rot"] * (2 * heads), 1), (["plain"] * heads + ["silu"] * heads, 1)],
                      tm=1024, tn=2 * C_WIDTH, vec=vec_c, tabs=rot_c)
        o_c = _retention(c_all, c_gn[l], decay_tabs)

        merged = _branch(u, w_tail, [tail_col(seg) for seg in (10, 11, 12)], o_a, o_b, o_c, wa, wb, wc, l)
        x, u2 = _outproj(merged, wo, l, x, mlp_norm[l])
        if l + 1 < depth:
            x, u = _mlp(u2, wu, wd, l, x, next_gain=mix_norm[l + 1])
        else:
            x = _mlp(u2, wu, wd, l, x)
    return x.reshape(b, s, d)
```

```python
import functools
import math

import jax
import jax.numpy as jnp
import numpy as np
from jax import lax
from jax.experimental import pallas as pl
from jax.experimental.pallas import tpu as pltpu

F32 = jnp.float32
BF16 = jnp.bfloat16

D_MODEL = 2048
EPS = 1e-6
NEG_INF = -1e30
BLOCK = 128
LANES = 128
HEAD_DIM = 64
ROPE_DIM = HEAD_DIM // 4
ROPE_THETA = 500000.0
A_DILATED = ((128, 1), (512, 4), (2048, 16))
A_GROUPS = len(A_DILATED)
A_HEADS_PER_GROUP = 6
A_GROUP_WIDTH = A_HEADS_PER_GROUP * HEAD_DIM
A_PAIRS = A_GROUP_WIDTH // LANES
A_SUPER = BLOCK * max(d for _, d in A_DILATED)
B_WINDOW = 128
B_Q_HEADS = 16
B_KV_HEADS = 4
B_REP = B_Q_HEADS // B_KV_HEADS
B_Q_WIDTH = B_Q_HEADS * HEAD_DIM
B_KV_WIDTH = B_KV_HEADS * HEAD_DIM
B_PAIRS = B_Q_WIDTH // LANES
C_HEADS = 8
C_HEAD_DIM = 128
C_WIDTH = C_HEADS * C_HEAD_DIM
C_ROT_THETA = 10000.0
RET_CHUNK = 256
RET_UNROLL = 8
D_FF = 4 * D_MODEL
IN_SIZES = (A_GROUPS * A_GROUP_WIDTH,) * 3 + (B_Q_WIDTH, B_KV_WIDTH, B_KV_WIDTH) + (C_WIDTH,) * 4 + (D_MODEL,) * 3

VMEM_LIMIT = 56 * 1024 * 1024
LOOP_UNROLL = 16
LOG2E = math.log2(math.e)
PROJ_UNROLL = 2
PROJ_SUB = 512


def _params(*sem):
    return pltpu.CompilerParams(dimension_semantics=sem, vmem_limit_bytes=VMEM_LIMIT)


def _sigmoid(a):
    return 1.0 / (1.0 + jnp.exp(-a))


def _rmsnorm_kernel(x_ref, g_ref, o_ref):
    x = x_ref[...]
    ms = jnp.mean(x * x, axis=-1, keepdims=True)
    o_ref[...] = (x * lax.rsqrt(ms + EPS) * g_ref[...]).astype(o_ref.dtype)


def _rmsnorm(x, g, tm=512):
    s, d = x.shape
    return pl.pallas_call(
        _rmsnorm_kernel,
        grid=(s // tm,),
        in_specs=[pl.BlockSpec((tm, d), lambda i: (i, 0)),
                  pl.BlockSpec((1, d), lambda i: (0, 0))],
        out_specs=pl.BlockSpec((tm, d), lambda i: (i, 0)),
        out_shape=jax.ShapeDtypeStruct((s, d), BF16),
        compiler_params=_params("parallel"),
        name="rmsnorm",
    )(x, g.reshape(1, d))


PERM_STAGE = 4


def _store_residue_major(o_ref, scr_a, scr_b, y, c, sl, row0, perm):
    sub = y.shape[0]
    chunk_rows = BLOCK * perm
    base = row0 // chunk_rows * chunk_rows + row0 % chunk_rows // perm
    n = sub // perm
    if isinstance(base, int):
        dst = lambda r: slice(base + r * BLOCK, base + r * BLOCK + n)
    else:
        dst = lambda r: pl.ds(pl.multiple_of(base + r * BLOCK, n), n)
    scr_a[c] = y
    if perm <= PERM_STAGE:
        for r in range(perm):
            o_ref[dst(r), sl] = scr_a[c, pl.ds(r, n, stride=perm), :].astype(o_ref.dtype)
        return
    outer = perm // PERM_STAGE
    assert outer <= PERM_STAGE and perm % PERM_STAGE == 0
    part = sub // PERM_STAGE
    for r0 in range(PERM_STAGE):
        scr_b[c, r0 * part:(r0 + 1) * part, :] = scr_a[c, pl.ds(r0, part, stride=PERM_STAGE), :]
    for r0 in range(PERM_STAGE):
        for r1 in range(outer):
            o_ref[dst(r1 * PERM_STAGE + r0), sl] = (
                scr_b[c, pl.ds(r0 * part + r1, n, stride=outer), :].astype(o_ref.dtype))


def _proj_kernel(tiles, sub, *refs):
    refs = list(refs)
    any_perm = any(p > 1 for _, p in tiles)
    scr_b = refs.pop() if any_perm else None
    scr_a = refs.pop() if any_perm else None
    o_ref = refs.pop()
    u_ref, w_ref = refs[0], refs[1]
    rest = refs[2:]
    all_kinds = [k for kinds, _ in tiles for k in kinds]
    need_tab = any(k in ("qk", "rot") for k in all_kinds)
    vec_ref = rest.pop(0) if need_tab else None
    cos_ref, sin_ref = (rest.pop(0), rest.pop(0)) if need_tab else (None, None)
    bd_ref = rest.pop(0) if "qk" in all_kinds else None
    dim = lax.broadcasted_iota(jnp.int32, (sub, LANES), 1) & (HEAD_DIM - 1)
    half = ROPE_DIM // 2

    def tile_body(kinds, perm):
        n_qk = sum(k == "qk" for k in kinds)
        assert n_qk % 2 == 0 and all(k == "qk" for k in kinds[:n_qk])
        n_sub = u_ref.shape[0] // sub

        def sub_tile(m):
            row0 = m * sub if isinstance(m, int) else pl.multiple_of(m * sub, sub)
            rows = pl.ds(row0, sub)
            acc = jnp.dot(u_ref[rows, :], w_ref[...], preferred_element_type=F32)
            ss = []
            for c in range(n_qk // 2):
                a2 = acc[:, 2 * c * LANES:2 * (c + 1) * LANES]
                s2 = jnp.dot((a2 * a2).astype(BF16), bd_ref[...], preferred_element_type=F32)
                ss += [s2[:, :LANES], s2[:, LANES:]]
            for c, kind in enumerate(kinds):
                sl = slice(c * LANES, (c + 1) * LANES)
                a = acc[:, sl]
                if kind == "qk":
                    y = a * lax.rsqrt(ss[c] * (1.0 / HEAD_DIM) + EPS) * vec_ref[:, sl]
                    turned = jnp.where(dim < half, -pltpu.roll(y, LANES - half, 1),
                                       jnp.where(dim < ROPE_DIM, pltpu.roll(y, half, 1), 0.0))
                    y = y * cos_ref[rows, :] + turned * sin_ref[rows, :]
                elif kind == "rot":
                    y = (a * cos_ref[rows, :] + pltpu.roll(a, LANES // 2, 1) * sin_ref[rows, :]) * vec_ref[:, sl]
                elif kind == "sigmoid":
                    y = _sigmoid(a)
                elif kind == "silu":
                    y = a * _sigmoid(a)
                elif kind == "relu2":
                    y = jnp.square(jnp.maximum(a, 0.0))
                else:
                    y = a
                if perm == 1:
                    o_ref[rows, sl] = y.astype(o_ref.dtype)
                else:
                    _store_residue_major(o_ref, scr_a, scr_b, y, c, sl, row0, perm)

        if n_sub <= PROJ_UNROLL:
            for m in range(n_sub):
                sub_tile(m)
        else:
            def pair(t, carry):
                for k in range(PROJ_UNROLL):
                    sub_tile(t * PROJ_UNROLL + k)
                return carry
            lax.fori_loop(0, n_sub // PROJ_UNROLL, pair, 0)

    if len(tiles) == 1:
        tile_body(*tiles[0])
    else:
        for j, (kinds, perm) in enumerate(tiles):
            pl.when(pl.program_id(1) == j)(functools.partial(tile_body, kinds, perm))


def _proj(u, w, layer, col0, tiles, tm, tn, vec=None, tabs=None, bd=None, sub=PROJ_SUB):
    s, k = u.shape
    tiles = tuple((tuple(kinds), perm) for kinds, perm in tiles)
    n = len(tiles) * tn
    assert col0 % tn == 0 and s % tm == 0 and tm % sub == 0
    for kinds, perm in tiles:
        assert len(kinds) == tn // LANES
        assert perm == 1 or (sub % perm == 0 and (BLOCK * perm) % sub == 0 and tm % (BLOCK * perm) == 0)
    if all(t == tiles[0] for t in tiles):
        tiles = tiles[:1]
    j0 = col0 // tn
    args = [u, w]
    in_specs = [pl.BlockSpec((tm, k), lambda i, j: (i, 0)),
                pl.BlockSpec((None, k, tn), lambda i, j: (layer, 0, j0 + j))]
    if tabs is not None:
        args.append(vec.reshape(1, n).astype(F32))
        in_specs.append(pl.BlockSpec((1, tn), lambda i, j: (0, j)))
        for t in tabs:
            args.append(t)
            in_specs.append(pl.BlockSpec((tm, LANES), lambda i, j: (i, 0)))
    if bd is not None:
        args.append(bd)
        in_specs.append(pl.BlockSpec(bd.shape, lambda i, j: (0, 0)))
    scratch = []
    if any(p > 1 for _, p in tiles):
        scratch = [pltpu.VMEM((tn // LANES, sub, LANES), F32)] * 2
    return pl.pallas_call(
        functools.partial(_proj_kernel, tiles, sub),
        grid=(s // tm, n // tn),
        in_specs=in_specs,
        out_specs=pl.BlockSpec((tm, tn), lambda i, j: (i, j)),
        out_shape=jax.ShapeDtypeStruct((s, n), BF16),
        scratch_shapes=scratch,
        compiler_params=_params("parallel", "arbitrary"),
        name="proj_" + tiles[0][0][0],
    )(*args)


def _half_masks(rows):
    lane = lax.broadcasted_iota(jnp.int32, (rows, LANES), 1)
    lo = jnp.where(lane < HEAD_DIM, 1.0, 0.0)
    return lo.astype(BF16), (1.0 - lo).astype(BF16)


def _pair_attend(q, k2, v2, bias, sink_key=False):
    if sink_key:
        rows = 16
        keep = jnp.where(lax.broadcasted_iota(jnp.int32, (rows, LANES), 0) > 0, 1.0, 0.0).astype(BF16)
        k2 = jnp.concatenate([k2[:rows] * keep, k2[rows:]], axis=0)
        v2 = jnp.concatenate([v2[:rows] * keep, v2[rows:]], axis=0)
    lo_b, hi_b = _half_masks(BLOCK)
    q2 = jnp.concatenate([q * lo_b, q * hi_b], axis=0)
    s2 = lax.dot_general(q2, k2, (((1,), (1,)), ((), ())), preferred_element_type=F32) + bias
    m2 = jnp.max(s2, axis=-1, keepdims=True)
    p2 = jnp.exp2(s2 - m2).astype(BF16)
    vx = jnp.concatenate([v2, jnp.ones_like(v2)], axis=1)
    r2 = jnp.dot(p2, vx, preferred_element_type=F32)
    lo = lax.broadcasted_iota(jnp.int32, (BLOCK, LANES), 1) < HEAD_DIM
    num = jnp.where(lo, r2[:BLOCK, :LANES], r2[BLOCK:, :LANES])
    den = jnp.where(lo, r2[:BLOCK, LANES:], r2[BLOCK:, LANES:])
    mx = jnp.where(lo, m2[:BLOCK], m2[BLOCK:])
    return num, den, mx


def _fill_bias(bias_ref, max_dist, sinks=None):
    row = lax.broadcasted_iota(jnp.int32, (2 * BLOCK, 2 * BLOCK), 0)
    kj = lax.broadcasted_iota(jnp.int32, (2 * BLOCK, 2 * BLOCK), 1)
    dist = (row & (BLOCK - 1)) + BLOCK - kj
    band = jnp.where(dist >= 0, jnp.where(dist <= max_dist, 0.0, NEG_INF), NEG_INF)
    first = jnp.where(kj >= BLOCK, band, NEG_INF)
    if sinks is not None:
        assert max_dist < BLOCK
        sink = jnp.where(row < BLOCK, sinks[0], sinks[1])
        band = jnp.where(kj == 0, sink, band)
        first = jnp.where(kj == 0, sink, first)
    bias_ref[0] = band
    bias_ref[1] = first


def _fill_ext(ext_ref, prev_ref, cur_ref):
    p = prev_ref.shape[0]
    ext_ref[0:p, :] = prev_ref[...]
    ext_ref[p:, :] = cur_ref[...]


def _attn_a_kernel(*refs):
    ins, (o_ref,), scr = refs[:15], refs[15:16], refs[16:]
    sb = pl.program_id(0)
    o_scr, l_scr, bias_ref = scr[0], scr[1], scr[2]
    _fill_bias(bias_ref, BLOCK)
    for g, (_, dil) in enumerate(A_DILATED):
        q_ref, k_ref, v_ref, kp_ref, vp_ref = ins[5 * g:5 * g + 5]
        kx_ref, vx_ref = scr[3 + 2 * g], scr[4 + 2 * g]
        _fill_ext(kx_ref, kp_ref, k_ref)
        _fill_ext(vx_ref, vp_ref, v_ref)
        prev_rows = BLOCK * dil

        def body(b, carry, g=g, dil=dil, q_ref=q_ref, kx_ref=kx_ref, vx_ref=vx_ref, prev_rows=prev_rows):
            row = pl.multiple_of(b * BLOCK, BLOCK)
            q = q_ref[pl.ds(row, BLOCK), :]
            k2 = jnp.concatenate([kx_ref[pl.ds(row, BLOCK), :],
                                  kx_ref[pl.ds(row + prev_rows, BLOCK), :]], axis=0)
            v2 = jnp.concatenate([vx_ref[pl.ds(row, BLOCK), :],
                                  vx_ref[pl.ds(row + prev_rows, BLOCK), :]], axis=0)
            no_prev = jnp.logical_and(sb == 0, b < dil)
            num, den, mx = _pair_attend(q, k2, v2, bias_ref[no_prev.astype(jnp.int32)])
            if dil == 1:
                dst = pl.ds(row, BLOCK)
            else:
                chunk = b // dil
                res = b - chunk * dil
                dst = pl.ds(chunk * prev_rows + res, BLOCK, stride=dil)
            o_scr[g, dst, :] = num / den
            l_scr[g, dst, :] = mx + jnp.log2(den)
            return carry

        lax.fori_loop(0, A_SUPER // BLOCK, body, 0, unroll=LOOP_UNROLL)

    l0, l1, l2 = l_scr[0], l_scr[1], l_scr[2]
    lm = jnp.maximum(jnp.maximum(l0, l1), l2)
    w0, w1, w2 = jnp.exp2(l0 - lm), jnp.exp2(l1 - lm), jnp.exp2(l2 - lm)
    o = (w0 * o_scr[0] + w1 * o_scr[1] + w2 * o_scr[2]) / (w0 + w1 + w2)
    o_ref[...] = o.astype(o_ref.dtype)


def _attn_a(qkv):
    s = qkv.shape[0]
    args, in_specs = [], []
    scratch = [pltpu.VMEM((A_GROUPS, A_SUPER, LANES), F32),
               pltpu.VMEM((A_GROUPS, A_SUPER, LANES), F32),
               pltpu.VMEM((2, 2 * BLOCK, 2 * BLOCK), F32)]
    for g, (_, dil) in enumerate(A_DILATED):
        prev_rows = BLOCK * dil
        per = A_SUPER // prev_rows
        col = 3 * A_PAIRS * g

        def cur_map(off):
            return lambda i, p: (i, off + p)

        def prev_map(off, per=per):
            return lambda i, p: (jnp.maximum(i * per - 1, 0), off + p)

        args += [qkv] * 5
        in_specs += [pl.BlockSpec((A_SUPER, LANES), cur_map(col)),
                     pl.BlockSpec((A_SUPER, LANES), cur_map(col + A_PAIRS)),
                     pl.BlockSpec((A_SUPER, LANES), cur_map(col + 2 * A_PAIRS)),
                     pl.BlockSpec((prev_rows, LANES), prev_map(col + A_PAIRS)),
                     pl.BlockSpec((prev_rows, LANES), prev_map(col + 2 * A_PAIRS))]
        scratch += [pltpu.VMEM((prev_rows + A_SUPER, LANES), BF16)] * 2
    return pl.pallas_call(
        _attn_a_kernel,
        grid=(s // A_SUPER, A_PAIRS),
        in_specs=in_specs,
        out_specs=pl.BlockSpec((A_SUPER, LANES), lambda i, p: (i, p)),
        out_shape=jax.ShapeDtypeStruct((s, A_GROUP_WIDTH), BF16),
        scratch_shapes=scratch,
        compiler_params=_params("parallel", "parallel"),
        name="attn_a",
    )(*args)


def _attn_b_kernel(q_ref, k_ref, v_ref, kp_ref, vp_ref, sink_ref, o_ref, kx_ref, vx_ref, bias_ref):
    i = pl.program_id(0)
    p = pl.program_id(1)
    _fill_bias(bias_ref, B_WINDOW - 1, sinks=(sink_ref[2 * p], sink_ref[2 * p + 1]))
    _fill_ext(kx_ref, kp_ref, k_ref)
    _fill_ext(vx_ref, vp_ref, v_ref)

    def body(b, carry):
        row = pl.multiple_of(b * BLOCK, BLOCK)
        q = q_ref[pl.ds(row, BLOCK), :]
        k2 = kx_ref[pl.ds(row, 2 * BLOCK), :]
        v2 = vx_ref[pl.ds(row, 2 * BLOCK), :]
        no_prev = jnp.logical_and(i == 0, b == 0)
        num, den, _ = _pair_attend(q, k2, v2, bias_ref[no_prev.astype(jnp.int32)], sink_key=True)
        o_ref[pl.ds(row, BLOCK), :] = (num / den).astype(o_ref.dtype)
        return carry

    lax.fori_loop(0, q_ref.shape[0] // BLOCK, body, 0, unroll=LOOP_UNROLL)


def _attn_b(qkv, sinks, tq=2048):
    s = qkv.shape[0]
    per = tq // BLOCK
    kv = lambda p: p // (B_PAIRS // 2)
    k0 = B_PAIRS
    v0 = B_PAIRS + B_KV_WIDTH // LANES
    return pl.pallas_call(
        _attn_b_kernel,
        grid=(s // tq, B_PAIRS),
        in_specs=[pl.BlockSpec((tq, LANES), lambda i, p: (i, p)),
                  pl.BlockSpec((tq, LANES), lambda i, p: (i, k0 + kv(p))),
                  pl.BlockSpec((tq, LANES), lambda i, p: (i, v0 + kv(p))),
                  pl.BlockSpec((BLOCK, LANES), lambda i, p: (jnp.maximum(i * per - 1, 0), k0 + kv(p))),
                  pl.BlockSpec((BLOCK, LANES), lambda i, p: (jnp.maximum(i * per - 1, 0), v0 + kv(p))),
                  pl.BlockSpec(memory_space=pltpu.SMEM)],
        out_specs=pl.BlockSpec((tq, LANES), lambda i, p: (i, p)),
        out_shape=jax.ShapeDtypeStruct((s, B_Q_WIDTH), BF16),
        scratch_shapes=[pltpu.VMEM((BLOCK + tq, LANES), BF16)] * 2
                       + [pltpu.VMEM((2, 2 * BLOCK, 2 * BLOCK), F32)],
        compiler_params=_params("parallel", "parallel"),
        name="attn_b",
    )(qkv, qkv, qkv, qkv, qkv, sinks)


def _retention_kernel(q_ref, k_ref, v_ref, g_ref, gn_ref, decay_ref, qd_ref, kd_ref, cd_ref, o_ref, state_ref):
    i = pl.program_id(0)
    h = pl.program_id(1)

    @pl.when(i == 0)
    def _():
        state_ref[h] = jnp.zeros((C_HEAD_DIM, C_HEAD_DIM), F32)

    gn = gn_ref[...]

    def body(n, state):
        row = pl.multiple_of(n * RET_CHUNK, RET_CHUNK)
        q = q_ref[pl.ds(row, RET_CHUNK), :]
        k = k_ref[pl.ds(row, RET_CHUNK), :]
        v = v_ref[pl.ds(row, RET_CHUNK), :]
        s = lax.dot_general(q, k, (((1,), (1,)), ((), ())), preferred_element_type=F32) * decay_ref[0]
        inner = jnp.dot(s.astype(BF16), v, preferred_element_type=F32)
        cross = jnp.dot(q, state.astype(BF16), preferred_element_type=F32) * qd_ref[0]
        kd_t = (k.astype(F32) * kd_ref[0]).T.astype(BF16)
        new_state = cd_ref[0] * state + jnp.dot(kd_t, v, preferred_element_type=F32)
        y = inner + cross
        y = y * lax.rsqrt(jnp.mean(y * y, axis=-1, keepdims=True) + EPS)
        o_ref[pl.ds(row, RET_CHUNK), :] = (g_ref[pl.ds(row, RET_CHUNK), :].astype(F32) * (y * gn)).astype(o_ref.dtype)
        return new_state

    state_ref[h] = lax.fori_loop(0, q_ref.shape[0] // RET_CHUNK, body, state_ref[h], unroll=RET_UNROLL)


def _retention(c_all, c_gn, tabs, tq=2048):
    s = c_all.shape[0]
    v_off = 2 * C_HEADS
    g_off = 3 * C_HEADS
    tab = lambda t: pl.BlockSpec((1,) + t.shape[1:], lambda i, h: (h, 0, 0))
    return pl.pallas_call(
        _retention_kernel,
        grid=(s // tq, C_HEADS),
        in_specs=[pl.BlockSpec((tq, LANES), lambda i, h: (i, h)),
                  pl.BlockSpec((tq, LANES), lambda i, h: (i, C_HEADS + h)),
                  pl.BlockSpec((tq, LANES), lambda i, h: (i, v_off + h)),
                  pl.BlockSpec((tq, LANES), lambda i, h: (i, g_off + h)),
                  pl.BlockSpec((1, LANES), lambda i, h: (0, h))] + [tab(t) for t in tabs],
        out_specs=pl.BlockSpec((tq, LANES), lambda i, h: (i, h)),
        out_shape=jax.ShapeDtypeStruct((s, C_WIDTH), BF16),
        scratch_shapes=[pltpu.VMEM((C_HEADS, C_HEAD_DIM, C_HEAD_DIM), F32)],
        compiler_params=_params("arbitrary", "arbitrary"),
        name="retention",
    )(c_all, c_all, c_all, c_all, c_gn.reshape(1, C_WIDTH).astype(F32), *tabs)


def _branch_kernel(sub, u_ref, ga_ref, gb_ref, gc_ref, oa_ref, ob_ref, oc_ref, wa_ref, wb_ref, wc_ref, o_ref):
    for m in range(u_ref.shape[0] // sub):
        rows = slice(m * sub, (m + 1) * sub)
        u = u_ref[rows, :]
        total = None
        for g_ref, o_in, w_ref in ((ga_ref, oa_ref, wa_ref), (gb_ref, ob_ref, wb_ref), (gc_ref, oc_ref, wc_ref)):
            gate = _sigmoid(jnp.dot(u, g_ref[...], preferred_element_type=F32))
            term = gate * jnp.dot(o_in[rows, :], w_ref[...], preferred_element_type=F32)
            total = term if total is None else total + term
        o_ref[rows, :] = total.astype(o_ref.dtype)


def _branch(u, w_in_pad, gate_cols, o_a, o_b, o_c, w_a, w_b, w_c, layer, tm=1024, tn=512, sub=PROJ_SUB):
    s, k = u.shape
    n = w_a.shape[2]
    act = lambda width: pl.BlockSpec((tm, width), lambda i, j: (i, 0))
    wgt = lambda width: pl.BlockSpec((None, width, tn), lambda i, j: (layer, 0, j))
    assert all(c % tn == 0 for c in gate_cols)
    gate = lambda col: pl.BlockSpec((None, k, tn), lambda i, j: (layer, 0, col // tn + j))
    return pl.pallas_call(
        functools.partial(_branch_kernel, sub),
        grid=(s // tm, n // tn),
        in_specs=[act(k)] + [gate(c) for c in gate_cols]
                 + [act(o_a.shape[1]), act(o_b.shape[1]), act(o_c.shape[1]),
                    wgt(w_a.shape[1]), wgt(w_b.shape[1]), wgt(w_c.shape[1])],
        out_specs=pl.BlockSpec((tm, tn), lambda i, j: (i, j)),
        out_shape=jax.ShapeDtypeStruct((s, n), BF16),
        compiler_params=_params("parallel", "arbitrary"),
        name="branch",
    )(u, w_in_pad, w_in_pad, w_in_pad, o_a, o_b, o_c, w_a, w_b, w_c)


def _outproj_kernel(m_ref, w_ref, x_ref, g_ref, x1_ref, u_ref):
    x1 = x_ref[...] + jnp.dot(m_ref[...], w_ref[...], preferred_element_type=F32)
    x1_ref[...] = x1
    ms = jnp.mean(x1 * x1, axis=-1, keepdims=True)
    u_ref[...] = (x1 * lax.rsqrt(ms + EPS) * g_ref[...]).astype(u_ref.dtype)


def _outproj(merged, w_out, layer, x, g, tm=512):
    s, d = x.shape
    return pl.pallas_call(
        _outproj_kernel,
        grid=(s // tm,),
        in_specs=[pl.BlockSpec((tm, d), lambda i: (i, 0)),
                  pl.BlockSpec((None, d, d), lambda i: (layer, 0, 0), pipeline_mode=pl.Buffered(1)),
                  pl.BlockSpec((tm, d), lambda i: (i, 0)),
                  pl.BlockSpec((1, d), lambda i: (0, 0))],
        out_specs=[pl.BlockSpec((tm, d), lambda i: (i, 0)),
                   pl.BlockSpec((tm, d), lambda i: (i, 0))],
        out_shape=[jax.ShapeDtypeStruct((s, d), F32), jax.ShapeDtypeStruct((s, d), BF16)],
        compiler_params=_params("parallel"),
        name="outproj",
    )(merged, w_out, x, g.reshape(1, d))


def _mlp_kernel(u_ref, wu_ref, wd_ref, x_ref, *rest):
    g_ref, o_ref, un_ref = rest if len(rest) == 3 else (None, rest[0], None)
    f = pl.program_id(1)

    @pl.when(f == 0)
    def _():
        o_ref[...] = x_ref[...]

    h = jnp.dot(u_ref[...], wu_ref[...], preferred_element_type=F32)
    h = jnp.square(jnp.maximum(h, 0.0)).astype(BF16)
    o_ref[...] += jnp.dot(h, wd_ref[...], preferred_element_type=F32)

    if un_ref is not None:
        @pl.when(f == pl.num_programs(1) - 1)
        def _():
            y = o_ref[...]
            ms = jnp.mean(y * y, axis=-1, keepdims=True)
            un_ref[...] = (y * lax.rsqrt(ms + EPS) * g_ref[...]).astype(un_ref.dtype)


def _mlp(u, w_up, w_down, layer, x, next_gain=None, tm=512, tf=1024):
    s, d = x.shape
    f = w_up.shape[2]
    row = pl.BlockSpec((tm, d), lambda i, j: (i, 0))
    args = [u, w_up, w_down, x]
    in_specs = [row,
                pl.BlockSpec((None, d, tf), lambda i, j: (layer, 0, j)),
                pl.BlockSpec((None, tf, d), lambda i, j: (layer, j, 0)),
                row]
    out_specs, out_shape = row, jax.ShapeDtypeStruct((s, d), F32)
    if next_gain is not None:
        args.append(next_gain.reshape(1, d))
        in_specs.append(pl.BlockSpec((1, d), lambda i, j: (0, 0)))
        out_specs, out_shape = [row, row], [out_shape, jax.ShapeDtypeStruct((s, d), BF16)]
    return pl.pallas_call(
        _mlp_kernel,
        grid=(s // tm, f // tf),
        in_specs=in_specs,
        out_specs=out_specs,
        out_shape=out_shape,
        compiler_params=_params("parallel", "arbitrary"),
        name="mlp",
    )(*args)


def _rope_tables(pos):
    half = ROPE_DIM // 2
    inv = ROPE_THETA ** (-jnp.arange(half, dtype=F32) / half)
    ang = pos.astype(F32)[:, None] * inv[None, :]
    cos, sin = jnp.cos(ang), jnp.sin(ang)
    n = pos.shape[0]
    pad = HEAD_DIM - ROPE_DIM
    t_cos = jnp.concatenate([cos, cos, jnp.ones((n, pad), F32)], axis=1)
    t_sin = jnp.concatenate([sin, sin, jnp.zeros((n, pad), F32)], axis=1)
    return tuple(jnp.tile(t, (1, LANES // HEAD_DIM)) for t in (t_cos, t_sin))


def _retention_rot_tables(pos):
    half = C_HEAD_DIM // 2
    inv = C_ROT_THETA ** (-jnp.arange(half, dtype=F32) / half)
    ang = pos.astype(F32)[:, None] * inv[None, :]
    cos, sin = jnp.cos(ang), jnp.sin(ang)
    return jnp.concatenate([cos, cos], axis=1), jnp.concatenate([-sin, sin], axis=1)


def _retention_decay_tables():
    c = RET_CHUNK
    log_g = jnp.log1p(-(2.0 ** (-5.0 - jnp.arange(C_HEADS, dtype=F32))))
    i = jnp.arange(c, dtype=F32)
    rel = i[:, None] - i[None, :]
    decay = jnp.where(rel >= 0, jnp.exp(log_g[:, None, None] * jnp.maximum(rel, 0.0)), 0.0)
    rows = lambda t: jnp.broadcast_to(t[:, :, None], (C_HEADS, c, C_HEAD_DIM))
    q_decay = rows(jnp.exp(log_g[:, None] * (i + 1.0)[None, :]))
    k_decay = rows(jnp.exp(log_g[:, None] * (c - 1 - i)[None, :]))
    chunk_decay = jnp.broadcast_to(jnp.exp(log_g * c)[:, None, None], (C_HEADS, C_HEAD_DIM, C_HEAD_DIM))
    return decay, q_decay, k_decay, chunk_decay


def _b_head_order():
    order = []
    for p in range(B_PAIRS):
        g2, a = divmod(p, B_PAIRS // 2)
        order += [B_REP * (2 * g2) + a, B_REP * (2 * g2 + 1) + a]
    return order


PREP_ROWS = 128


def _regroup_in_kernel(pieces, w_ref, *o_refs):
    for out, dst, src, width in pieces:
        o_refs[out][:, dst:dst + width] = w_ref[:, src:src + width].astype(BF16)


def _regroup_in(w_in, b_order):
    depth, k, _ = w_in.shape
    start = [sum(IN_SIZES[:i]) for i in range(len(IN_SIZES) + 1)]
    pieces = []
    for g in range(A_GROUPS):
        for t in range(3):
            pieces.append((0, (3 * g + t) * A_GROUP_WIDTH, start[t] + g * A_GROUP_WIDTH, A_GROUP_WIDTH))
    for slot, h in enumerate(b_order):
        pieces.append((1, slot * HEAD_DIM, start[3] + h * HEAD_DIM, HEAD_DIM))
    pieces.append((1, B_Q_WIDTH, start[4], 2 * B_KV_WIDTH))
    pieces.append((2, 0, start[6], start[13] - start[6]))
    widths = (3 * A_GROUPS * A_GROUP_WIDTH, B_Q_WIDTH + 2 * B_KV_WIDTH, start[13] - start[6])
    return pl.pallas_call(
        functools.partial(_regroup_in_kernel, tuple(pieces)),
        grid=(depth, k // PREP_ROWS),
        in_specs=[pl.BlockSpec((None, PREP_ROWS, start[13]), lambda l, i: (l, i, 0))],
        out_specs=[pl.BlockSpec((None, PREP_ROWS, w), lambda l, i: (l, i, 0)) for w in widths],
        out_shape=[jax.ShapeDtypeStruct((depth, k, w), BF16) for w in widths],
        compiler_params=_params("parallel", "parallel"),
        name="regroup_in",
    )(w_in)


def _permute_rows_kernel(order, w_ref, o_ref):
    for slot, h in enumerate(order):
        o_ref[slot * HEAD_DIM:(slot + 1) * HEAD_DIM, :] = w_ref[h * HEAD_DIM:(h + 1) * HEAD_DIM, :].astype(BF16)


def _permute_head_rows(w, order):
    depth, rows, n = w.shape
    return pl.pallas_call(
        functools.partial(_permute_rows_kernel, tuple(order)),
        grid=(depth,),
        in_specs=[pl.BlockSpec((None, rows, n), lambda l: (l, 0, 0))],
        out_specs=pl.BlockSpec((None, rows, n), lambda l: (l, 0, 0)),
        out_shape=jax.ShapeDtypeStruct((depth, rows, n), BF16),
        compiler_params=_params("parallel"),
        name="permute_rows",
    )(w)


def kernel(x, mix_norm, w_in, a_q_norm, a_k_norm, b_q_norm, b_k_norm, b_sinks, c_gn,
           w_br_a, w_br_b, w_br_c, w_out, mlp_norm, w_up, w_down):
    b, s, d = x.shape
    assert b == 1 and d == D_MODEL and s % A_SUPER == 0
    depth = w_in.shape[0]
    x = x.reshape(s, d)
    pos = jnp.arange(s)
    rope = _rope_tables(pos)
    rot_c = _retention_rot_tables(pos)
    decay_tabs = _retention_decay_tables()
    width2 = 2 * LANES
    bd = (jnp.arange(width2)[:, None] // HEAD_DIM == jnp.arange(width2)[None, :] // HEAD_DIM).astype(BF16)
    b_order = _b_head_order()
    q_scale = HEAD_DIM ** -0.5 * LOG2E
    tile = jnp.tile

    w_a, w_b, w_tail = _regroup_in(w_in, b_order)
    tail_col = lambda seg: sum(IN_SIZES[6:seg])
    wa = w_br_a.astype(BF16)
    wb = _permute_head_rows(w_br_b, b_order)
    wc = w_br_c.astype(BF16)
    wo = w_out.astype(BF16)
    wu = w_up.astype(BF16)
    wd = w_down.astype(BF16)
    a_width = 3 * A_GROUP_WIDTH
    b_width = B_Q_WIDTH + 2 * B_KV_WIDTH

    vec_a = tile(jnp.concatenate([tile(a_q_norm * q_scale, (1, A_HEADS_PER_GROUP)),
                                  tile(a_k_norm, (1, A_HEADS_PER_GROUP)),
                                  jnp.ones((depth, A_GROUP_WIDTH), F32)], axis=1), (1, A_GROUPS))
    vec_b = jnp.concatenate([tile(b_q_norm * q_scale, (1, B_Q_HEADS)), tile(b_k_norm, (1, B_KV_HEADS)),
                             jnp.ones((depth, B_KV_WIDTH), F32)], axis=1)
    vec_c = jnp.concatenate([jnp.ones((C_WIDTH,), F32), jnp.full((C_WIDTH,), C_HEAD_DIM ** -0.5, F32),
                             jnp.ones((2 * C_WIDTH,), F32)])
    half = B_PAIRS // 2
    sinks = (b_sinks.astype(F32) * LOG2E).reshape(depth, 2, 2, half).transpose(0, 1, 3, 2).reshape(depth, B_Q_HEADS)
    assert np.arange(B_Q_HEADS).reshape(2, 2, half).transpose(0, 2, 1).reshape(-1).tolist() == b_order

    u = _rmsnorm(x, mix_norm[0])
    for l in range(depth):
        kinds = ["qk"] * (2 * A_PAIRS) + ["plain"] * A_PAIRS
        qkv = _proj(u, w_a, l, 0, [(kinds, dil) for _, dil in A_DILATED], tm=A_SUPER, tn=a_width,
                    vec=vec_a[l], tabs=rope, bd=bd)
        o_a = _attn_a(qkv)

        kinds = ["qk"] * ((B_Q_WIDTH + B_KV_WIDTH) // LANES) + ["plain"] * (B_KV_WIDTH // LANES)
        bqkv = _proj(u, w_b, l, 0, [(kinds, 1)], tm=1024, tn=b_width, vec=vec_b[l], tabs=rope, bd=bd)
        o_b = _attn_b(bqkv, sinks[l])

        heads = C_WIDTH // LANES
        c_all = _proj(u, w_tail, l, tail_col(6), [(["rot"] * (2 * heads), 1), (["plain"] * heads + ["silu"] * heads, 1)],
                      tm=1024, tn=2 * C_WIDTH, vec=vec_c, tabs=rot_c)
        o_c = _retention(c_all, c_gn[l], decay_tabs)

        merged = _branch(u, w_tail, [tail_col(seg) for seg in (10, 11, 12)], o_a, o_b, o_c, wa, wb, wc, l)
        x, u2 = _outproj(merged, wo, l, x, mlp_norm[l])
        if l + 1 < depth:
            x, u = _mlp(u2, wu, wd, l, x, next_gain=mix_norm[l + 1])
        else:
            x = _mlp(u2, wu, wd, l, x)
    return x.reshape(b, s, d)
```

```python
import functools
import math

import jax
import jax.numpy as jnp
import numpy as np
from jax import lax
from jax.experimental import pallas as pl
from jax.experimental.pallas import tpu as pltpu

F32 = jnp.float32
BF16 = jnp.bfloat16

D_MODEL = 2048
EPS = 1e-6
NEG_INF = -1e30
BLOCK = 128
LANES = 128
HEAD_DIM = 64
ROPE_DIM = HEAD_DIM // 4
ROPE_THETA = 500000.0
A_DILATED = ((128, 1), (512, 4), (2048, 16))
A_GROUPS = len(A_DILATED)
A_HEADS_PER_GROUP = 6
A_GROUP_WIDTH = A_HEADS_PER_GROUP * HEAD_DIM
A_PAIRS = A_GROUP_WIDTH // LANES
A_SUPER = BLOCK * max(d for _, d in A_DILATED)
B_WINDOW = 128
B_Q_HEADS = 16
B_KV_HEADS = 4
B_REP = B_Q_HEADS // B_KV_HEADS
B_Q_WIDTH = B_Q_HEADS * HEAD_DIM
B_KV_WIDTH = B_KV_HEADS * HEAD_DIM
B_PAIRS = B_Q_WIDTH // LANES
C_HEADS = 8
C_HEAD_DIM = 128
C_WIDTH = C_HEADS * C_HEAD_DIM
C_ROT_THETA = 10000.0
RET_CHUNK = 256
RET_UNROLL = 8
D_FF = 4 * D_MODEL
IN_SIZES = (A_GROUPS * A_GROUP_WIDTH,) * 3 + (B_Q_WIDTH, B_KV_WIDTH, B_KV_WIDTH) + (C_WIDTH,) * 4 + (D_MODEL,) * 3

VMEM_LIMIT = 56 * 1024 * 1024
LOOP_UNROLL = 16
LOG2E = math.log2(math.e)
PROJ_UNROLL = 1
PROJ_SUB = 512


def _params(*sem):
    return pltpu.CompilerParams(dimension_semantics=sem, vmem_limit_bytes=VMEM_LIMIT)


def _sigmoid(a):
    return 1.0 / (1.0 + jnp.exp(-a))


def _rmsnorm_kernel(x_ref, g_ref, o_ref):
    x = x_ref[...]
    ms = jnp.mean(x * x, axis=-1, keepdims=True)
    o_ref[...] = (x * lax.rsqrt(ms + EPS) * g_ref[...]).astype(o_ref.dtype)


def _rmsnorm(x, g, tm=512):
    s, d = x.shape
    return pl.pallas_call(
        _rmsnorm_kernel,
        grid=(s // tm,),
        in_specs=[pl.BlockSpec((tm, d), lambda i: (i, 0)),
                  pl.BlockSpec((1, d), lambda i: (0, 0))],
        out_specs=pl.BlockSpec((tm, d), lambda i: (i, 0)),
        out_shape=jax.ShapeDtypeStruct((s, d), BF16),
        compiler_params=_params("parallel"),
        name="rmsnorm",
    )(x, g.reshape(1, d))


PERM_STAGE = 4


def _store_residue_major(o_ref, scr_a, scr_b, y, c, sl, row0, perm):
    sub = y.shape[0]
    chunk_rows = BLOCK * perm
    base = row0 // chunk_rows * chunk_rows + row0 % chunk_rows // perm
    n = sub // perm
    if isinstance(base, int):
        dst = lambda r: slice(base + r * BLOCK, base + r * BLOCK + n)
    else:
        dst = lambda r: pl.ds(pl.multiple_of(base + r * BLOCK, n), n)
    scr_a[c] = y
    if perm <= PERM_STAGE:
        for r in range(perm):
            o_ref[dst(r), sl] = scr_a[c, pl.ds(r, n, stride=perm), :].astype(o_ref.dtype)
        return
    outer = perm // PERM_STAGE
    assert outer <= PERM_STAGE and perm % PERM_STAGE == 0
    part = sub // PERM_STAGE
    for r0 in range(PERM_STAGE):
        scr_b[c, r0 * part:(r0 + 1) * part, :] = scr_a[c, pl.ds(r0, part, stride=PERM_STAGE), :]
    for r0 in range(PERM_STAGE):
        for r1 in range(outer):
            o_ref[dst(r1 * PERM_STAGE + r0), sl] = (
                scr_b[c, pl.ds(r0 * part + r1, n, stride=outer), :].astype(o_ref.dtype))


def _proj_kernel(tiles, sub, *refs):
    refs = list(refs)
    any_perm = any(p > 1 for _, p in tiles)
    scr_b = refs.pop() if any_perm else None
    scr_a = refs.pop() if any_perm else None
    o_ref = refs.pop()
    u_ref, w_ref = refs[0], refs[1]
    rest = refs[2:]
    all_kinds = [k for kinds, _ in tiles for k in kinds]
    need_tab = any(k in ("qk", "rot") for k in all_kinds)
    vec_ref = rest.pop(0) if need_tab else None
    cos_ref, sin_ref = (rest.pop(0), rest.pop(0)) if need_tab else (None, None)
    bd_ref = rest.pop(0) if "qk" in all_kinds else None
    dim = lax.broadcasted_iota(jnp.int32, (sub, LANES), 1) & (HEAD_DIM - 1)
    half = ROPE_DIM // 2

    def tile_body(kinds, perm):
        n_qk = sum(k == "qk" for k in kinds)
        assert n_qk % 2 == 0 and all(k == "qk" for k in kinds[:n_qk])
        n_sub = u_ref.shape[0] // sub

        def sub_tile(m):
            row0 = m * sub if isinstance(m, int) else pl.multiple_of(m * sub, sub)
            rows = pl.ds(row0, sub)
            acc = jnp.dot(u_ref[rows, :], w_ref[...], preferred_element_type=F32)
            ss = []
            for c in range(n_qk // 2):
                a2 = acc[:, 2 * c * LANES:2 * (c + 1) * LANES]
                s2 = jnp.dot((a2 * a2).astype(BF16), bd_ref[...], preferred_element_type=F32)
                ss += [s2[:, :LANES], s2[:, LANES:]]
            for c, kind in enumerate(kinds):
                sl = slice(c * LANES, (c + 1) * LANES)
                a = acc[:, sl]
                if kind == "qk":
                    y = a * lax.rsqrt(ss[c] * (1.0 / HEAD_DIM) + EPS) * vec_ref[:, sl]
                    turned = jnp.where(dim < half, -pltpu.roll(y, LANES - half, 1),
                                       jnp.where(dim < ROPE_DIM, pltpu.roll(y, half, 1), 0.0))
                    y = y * cos_ref[rows, :] + turned * sin_ref[rows, :]
                elif kind == "rot":
                    y = (a * cos_ref[rows, :] + pltpu.roll(a, LANES // 2, 1) * sin_ref[rows, :]) * vec_ref[:, sl]
                elif kind == "sigmoid":
                    y = _sigmoid(a)
                elif kind == "silu":
                    y = a * _sigmoid(a)
                elif kind == "relu2":
                    y = jnp.square(jnp.maximum(a, 0.0))
                else:
                    y = a
                if perm == 1:
                    o_ref[rows, sl] = y.astype(o_ref.dtype)
                else:
                    _store_residue_major(o_ref, scr_a, scr_b, y, c, sl, row0, perm)

        if n_sub <= PROJ_UNROLL:
            for m in range(n_sub):
                sub_tile(m)
        else:
            def pair(t, carry):
                for k in range(PROJ_UNROLL):
                    sub_tile(t * PROJ_UNROLL + k)
                return carry
            lax.fori_loop(0, n_sub // PROJ_UNROLL, pair, 0)

    if len(tiles) == 1:
        tile_body(*tiles[0])
    else:
        for j, (kinds, perm) in enumerate(tiles):
            pl.when(pl.program_id(1) == j)(functools.partial(tile_body, kinds, perm))


def _proj(u, w, layer, col0, tiles, tm, tn, vec=None, tabs=None, bd=None, sub=PROJ_SUB):
    s, k = u.shape
    tiles = tuple((tuple(kinds), perm) for kinds, perm in tiles)
    n = len(tiles) * tn
    assert col0 % tn == 0 and s % tm == 0 and tm % sub == 0
    for kinds, perm in tiles:
        assert len(kinds) == tn // LANES
        assert perm == 1 or (sub % perm == 0 and (BLOCK * perm) % sub == 0 and tm % (BLOCK * perm) == 0)
    if all(t == tiles[0] for t in tiles):
        tiles = tiles[:1]
    j0 = col0 // tn
    args = [u, w]
    in_specs = [pl.BlockSpec((tm, k), lambda i, j: (i, 0)),
                pl.BlockSpec((None, k, tn), lambda i, j: (layer, 0, j0 + j))]
    if tabs is not None:
        args.append(vec.reshape(1, n).astype(F32))
        in_specs.append(pl.BlockSpec((1, tn), lambda i, j: (0, j)))
        for t in tabs:
            args.append(t)
            in_specs.append(pl.BlockSpec((tm, LANES), lambda i, j: (i, 0)))
    if bd is not None:
        args.append(bd)
        in_specs.append(pl.BlockSpec(bd.shape, lambda i, j: (0, 0)))
    scratch = []
    if any(p > 1 for _, p in tiles):
        scratch = [pltpu.VMEM((tn // LANES, sub, LANES), F32)] * 2
    return pl.pallas_call(
        functools.partial(_proj_kernel, tiles, sub),
        grid=(s // tm, n // tn),
        in_specs=in_specs,
        out_specs=pl.BlockSpec((tm, tn), lambda i, j: (i, j)),
        out_shape=jax.ShapeDtypeStruct((s, n), BF16),
        scratch_shapes=scratch,
        compiler_params=_params("parallel", "arbitrary"),
        name="proj_" + tiles[0][0][0],
    )(*args)


def _half_masks(rows):
    lane = lax.broadcasted_iota(jnp.int32, (rows, LANES), 1)
    lo = jnp.where(lane < HEAD_DIM, 1.0, 0.0)
    return lo.astype(BF16), (1.0 - lo).astype(BF16)


def _pair_attend(q, k2, v2, bias, sink_key=False):
    if sink_key:
        rows = 16
        keep = jnp.where(lax.broadcasted_iota(jnp.int32, (rows, LANES), 0) > 0, 1.0, 0.0).astype(BF16)
        k2 = jnp.concatenate([k2[:rows] * keep, k2[rows:]], axis=0)
        v2 = jnp.concatenate([v2[:rows] * keep, v2[rows:]], axis=0)
    lo_b, hi_b = _half_masks(BLOCK)
    q2 = jnp.concatenate([q * lo_b, q * hi_b], axis=0)
    s2 = lax.dot_general(q2, k2, (((1,), (1,)), ((), ())), preferred_element_type=F32) + bias
    m2 = jnp.max(s2, axis=-1, keepdims=True)
    p2 = jnp.exp2(s2 - m2).astype(BF16)
    vx = jnp.concatenate([v2, jnp.ones_like(v2)], axis=1)
    r2 = jnp.dot(p2, vx, preferred_element_type=F32)
    lo = lax.broadcasted_iota(jnp.int32, (BLOCK, LANES), 1) < HEAD_DIM
    num = jnp.where(lo, r2[:BLOCK, :LANES], r2[BLOCK:, :LANES])
    den = jnp.where(lo, r2[:BLOCK, LANES:], r2[BLOCK:, LANES:])
    mx = jnp.where(lo, m2[:BLOCK], m2[BLOCK:])
    return num, den, mx


def _fill_bias(bias_ref, max_dist, sinks=None):
    row = lax.broadcasted_iota(jnp.int32, (2 * BLOCK, 2 * BLOCK), 0)
    kj = lax.broadcasted_iota(jnp.int32, (2 * BLOCK, 2 * BLOCK), 1)
    dist = (row & (BLOCK - 1)) + BLOCK - kj
    band = jnp.where(dist >= 0, jnp.where(dist <= max_dist, 0.0, NEG_INF), NEG_INF)
    first = jnp.where(kj >= BLOCK, band, NEG_INF)
    if sinks is not None:
        assert max_dist < BLOCK
        sink = jnp.where(row < BLOCK, sinks[0], sinks[1])
        band = jnp.where(kj == 0, sink, band)
        first = jnp.where(kj == 0, sink, first)
    bias_ref[0] = band
    bias_ref[1] = first


def _fill_ext(ext_ref, prev_ref, cur_ref):
    p = prev_ref.shape[0]
    ext_ref[0:p, :] = prev_ref[...]
    ext_ref[p:, :] = cur_ref[...]


def _attn_a_kernel(*refs):
    ins, (o_ref,), scr = refs[:15], refs[15:16], refs[16:]
    sb = pl.program_id(0)
    o_scr, l_scr, bias_ref = scr[0], scr[1], scr[2]
    _fill_bias(bias_ref, BLOCK)
    for g, (_, dil) in enumerate(A_DILATED):
        q_ref, k_ref, v_ref, kp_ref, vp_ref = ins[5 * g:5 * g + 5]
        kx_ref, vx_ref = scr[3 + 2 * g], scr[4 + 2 * g]
        _fill_ext(kx_ref, kp_ref, k_ref)
        _fill_ext(vx_ref, vp_ref, v_ref)
        prev_rows = BLOCK * dil

        def body(b, carry, g=g, dil=dil, q_ref=q_ref, kx_ref=kx_ref, vx_ref=vx_ref, prev_rows=prev_rows):
            row = pl.multiple_of(b * BLOCK, BLOCK)
            q = q_ref[pl.ds(row, BLOCK), :]
            k2 = jnp.concatenate([kx_ref[pl.ds(row, BLOCK), :],
                                  kx_ref[pl.ds(row + prev_rows, BLOCK), :]], axis=0)
            v2 = jnp.concatenate([vx_ref[pl.ds(row, BLOCK), :],
                                  vx_ref[pl.ds(row + prev_rows, BLOCK), :]], axis=0)
            no_prev = jnp.logical_and(sb == 0, b < dil)
            num, den, mx = _pair_attend(q, k2, v2, bias_ref[no_prev.astype(jnp.int32)])
            if dil == 1:
                dst = pl.ds(row, BLOCK)
            else:
                chunk = b // dil
                res = b - chunk * dil
                dst = pl.ds(chunk * prev_rows + res, BLOCK, stride=dil)
            o_scr[g, dst, :] = num / den
            l_scr[g, dst, :] = mx + jnp.log2(den)
            return carry

        lax.fori_loop(0, A_SUPER // BLOCK, body, 0, unroll=LOOP_UNROLL)

    l0, l1, l2 = l_scr[0], l_scr[1], l_scr[2]
    lm = jnp.maximum(jnp.maximum(l0, l1), l2)
    w0, w1, w2 = jnp.exp2(l0 - lm), jnp.exp2(l1 - lm), jnp.exp2(l2 - lm)
    o = (w0 * o_scr[0] + w1 * o_scr[1] + w2 * o_scr[2]) / (w0 + w1 + w2)
    o_ref[...] = o.astype(o_ref.dtype)


def _attn_a(qkv):
    s = qkv.shape[0]
    args, in_specs = [], []
    scratch = [pltpu.VMEM((A_GROUPS, A_SUPER, LANES), F32),
               pltpu.VMEM((A_GROUPS, A_SUPER, LANES), F32),
               pltpu.VMEM((2, 2 * BLOCK, 2 * BLOCK), F32)]
    for g, (_, dil) in enumerate(A_DILATED):
        prev_rows = BLOCK * dil
        per = A_SUPER // prev_rows
        col = 3 * A_PAIRS * g

        def cur_map(off):
            return lambda i, p: (i, off + p)

        def prev_map(off, per=per):
            return lambda i, p: (jnp.maximum(i * per - 1, 0), off + p)

        args += [qkv] * 5
        in_specs += [pl.BlockSpec((A_SUPER, LANES), cur_map(col)),
                     pl.BlockSpec((A_SUPER, LANES), cur_map(col + A_PAIRS)),
                     pl.BlockSpec((A_SUPER, LANES), cur_map(col + 2 * A_PAIRS)),
                     pl.BlockSpec((prev_rows, LANES), prev_map(col + A_PAIRS)),
                     pl.BlockSpec((prev_rows, LANES), prev_map(col + 2 * A_PAIRS))]
        scratch += [pltpu.VMEM((prev_rows + A_SUPER, LANES), BF16)] * 2
    return pl.pallas_call(
        _attn_a_kernel,
        grid=(s // A_SUPER, A_PAIRS),
        in_specs=in_specs,
        out_specs=pl.BlockSpec((A_SUPER, LANES), lambda i, p: (i, p)),
        out_shape=jax.ShapeDtypeStruct((s, A_GROUP_WIDTH), BF16),
        scratch_shapes=scratch,
        compiler_params=_params("parallel", "parallel"),
        name="attn_a",
    )(*args)


def _attn_b_kernel(q_ref, k_ref, v_ref, kp_ref, vp_ref, sink_ref, o_ref, kx_ref, vx_ref, bias_ref):
    i = pl.program_id(0)
    p = pl.program_id(1)
    _fill_bias(bias_ref, B_WINDOW - 1, sinks=(sink_ref[2 * p], sink_ref[2 * p + 1]))
    _fill_ext(kx_ref, kp_ref, k_ref)
    _fill_ext(vx_ref, vp_ref, v_ref)

    def body(b, carry):
        row = pl.multiple_of(b * BLOCK, BLOCK)
        q = q_ref[pl.ds(row, BLOCK), :]
        k2 = kx_ref[pl.ds(row, 2 * BLOCK), :]
        v2 = vx_ref[pl.ds(row, 2 * BLOCK), :]
        no_prev = jnp.logical_and(i == 0, b == 0)
        num, den, _ = _pair_attend(q, k2, v2, bias_ref[no_prev.astype(jnp.int32)], sink_key=True)
        o_ref[pl.ds(row, BLOCK), :] = (num / den).astype(o_ref.dtype)
        return carry

    lax.fori_loop(0, q_ref.shape[0] // BLOCK, body, 0, unroll=LOOP_UNROLL)


def _attn_b(qkv, sinks, tq=2048):
    s = qkv.shape[0]
    per = tq // BLOCK
    kv = lambda p: p // (B_PAIRS // 2)
    k0 = B_PAIRS
    v0 = B_PAIRS + B_KV_WIDTH // LANES
    return pl.pallas_call(
        _attn_b_kernel,
        grid=(s // tq, B_PAIRS),
        in_specs=[pl.BlockSpec((tq, LANES), lambda i, p: (i, p)),
                  pl.BlockSpec((tq, LANES), lambda i, p: (i, k0 + kv(p))),
                  pl.BlockSpec((tq, LANES), lambda i, p: (i, v0 + kv(p))),
                  pl.BlockSpec((BLOCK, LANES), lambda i, p: (jnp.maximum(i * per - 1, 0), k0 + kv(p))),
                  pl.BlockSpec((BLOCK, LANES), lambda i, p: (jnp.maximum(i * per - 1, 0), v0 + kv(p))),
                  pl.BlockSpec(memory_space=pltpu.SMEM)],
        out_specs=pl.BlockSpec((tq, LANES), lambda i, p: (i, p)),
        out_shape=jax.ShapeDtypeStruct((s, B_Q_WIDTH), BF16),
        scratch_shapes=[pltpu.VMEM((BLOCK + tq, LANES), BF16)] * 2
                       + [pltpu.VMEM((2, 2 * BLOCK, 2 * BLOCK), F32)],
        compiler_params=_params("parallel", "parallel"),
        name="attn_b",
    )(qkv, qkv, qkv, qkv, qkv, sinks)


def _retention_kernel(q_ref, k_ref, v_ref, g_ref, gn_ref, decay_ref, qd_ref, kd_ref, cd_ref, o_ref, state_ref):
    i = pl.program_id(0)
    h = pl.program_id(1)

    @pl.when(i == 0)
    def _():
        state_ref[h] = jnp.zeros((C_HEAD_DIM, C_HEAD_DIM), F32)

    gn = gn_ref[...]

    def body(n, state):
        row = pl.multiple_of(n * RET_CHUNK, RET_CHUNK)
        q = q_ref[pl.ds(row, RET_CHUNK), :]
        k = k_ref[pl.ds(row, RET_CHUNK), :]
        v = v_ref[pl.ds(row, RET_CHUNK), :]
        s = lax.dot_general(q, k, (((1,), (1,)), ((), ())), preferred_element_type=F32) * decay_ref[0]
        inner = jnp.dot(s.astype(BF16), v, preferred_element_type=F32)
        cross = jnp.dot(q, state.astype(BF16), preferred_element_type=F32) * qd_ref[0]
        kd_t = (k.astype(F32) * kd_ref[0]).T.astype(BF16)
        new_state = cd_ref[0] * state + jnp.dot(kd_t, v, preferred_element_type=F32)
        y = inner + cross
        y = y * lax.rsqrt(jnp.mean(y * y, axis=-1, keepdims=True) + EPS)
        o_ref[pl.ds(row, RET_CHUNK), :] = (g_ref[pl.ds(row, RET_CHUNK), :].astype(F32) * (y * gn)).astype(o_ref.dtype)
        return new_state

    state_ref[h] = lax.fori_loop(0, q_ref.shape[0] // RET_CHUNK, body, state_ref[h], unroll=RET_UNROLL)


def _retention(c_all, c_gn, tabs, tq=2048):
    s = c_all.shape[0]
    v_off = 2 * C_HEADS
    g_off = 3 * C_HEADS
    tab = lambda t: pl.BlockSpec((1,) + t.shape[1:], lambda i, h: (h, 0, 0))
    return pl.pallas_call(
        _retention_kernel,
        grid=(s // tq, C_HEADS),
        in_specs=[pl.BlockSpec((tq, LANES), lambda i, h: (i, h)),
                  pl.BlockSpec((tq, LANES), lambda i, h: (i, C_HEADS + h)),
                  pl.BlockSpec((tq, LANES), lambda i, h: (i, v_off + h)),
                  pl.BlockSpec((tq, LANES), lambda i, h: (i, g_off + h)),
                  pl.BlockSpec((1, LANES), lambda i, h: (0, h))] + [tab(t) for t in tabs],
        out_specs=pl.BlockSpec((tq, LANES), lambda i, h: (i, h)),
        out_shape=jax.ShapeDtypeStruct((s, C_WIDTH), BF16),
        scratch_shapes=[pltpu.VMEM((C_HEADS, C_HEAD_DIM, C_HEAD_DIM), F32)],
        compiler_params=_params("arbitrary", "arbitrary"),
        name="retention",
    )(c_all, c_all, c_all, c_all, c_gn.reshape(1, C_WIDTH).astype(F32), *tabs)


def _branch_kernel(sub, u_ref, ga_ref, gb_ref, gc_ref, oa_ref, ob_ref, oc_ref, wa_ref, wb_ref, wc_ref, o_ref):
    for m in range(u_ref.shape[0] // sub):
        rows = slice(m * sub, (m + 1) * sub)
        u = u_ref[rows, :]
        total = None
        for g_ref, o_in, w_ref in ((ga_ref, oa_ref, wa_ref), (gb_ref, ob_ref, wb_ref), (gc_ref, oc_ref, wc_ref)):
            gate = _sigmoid(jnp.dot(u, g_ref[...], preferred_element_type=F32))
            term = gate * jnp.dot(o_in[rows, :], w_ref[...], preferred_element_type=F32)
            total = term if total is None else total + term
        o_ref[rows, :] = total.astype(o_ref.dtype)


def _branch(u, w_in_pad, gate_cols, o_a, o_b, o_c, w_a, w_b, w_c, layer, tm=1024, tn=512, sub=PROJ_SUB):
    s, k = u.shape
    n = w_a.shape[2]
    act = lambda width: pl.BlockSpec((tm, width), lambda i, j: (i, 0))
    wgt = lambda width: pl.BlockSpec((None, width, tn), lambda i, j: (layer, 0, j))
    assert all(c % tn == 0 for c in gate_cols)
    gate = lambda col: pl.BlockSpec((None, k, tn), lambda i, j: (layer, 0, col // tn + j))
    return pl.pallas_call(
        functools.partial(_branch_kernel, sub),
        grid=(s // tm, n // tn),
        in_specs=[act(k)] + [gate(c) for c in gate_cols]
                 + [act(o_a.shape[1]), act(o_b.shape[1]), act(o_c.shape[1]),
                    wgt(w_a.shape[1]), wgt(w_b.shape[1]), wgt(w_c.shape[1])],
        out_specs=pl.BlockSpec((tm, tn), lambda i, j: (i, j)),
        out_shape=jax.ShapeDtypeStruct((s, n), BF16),
        compiler_params=_params("parallel", "arbitrary"),
        name="branch",
    )(u, w_in_pad, w_in_pad, w_in_pad, o_a, o_b, o_c, w_a, w_b, w_c)


def _outproj_kernel(m_ref, w_ref, x_ref, g_ref, x1_ref, u_ref):
    x1 = x_ref[...] + jnp.dot(m_ref[...], w_ref[...], preferred_element_type=F32)
    x1_ref[...] = x1
    ms = jnp.mean(x1 * x1, axis=-1, keepdims=True)
    u_ref[...] = (x1 * lax.rsqrt(ms + EPS) * g_ref[...]).astype(u_ref.dtype)


def _outproj(merged, w_out, layer, x, g, tm=512):
    s, d = x.shape
    return pl.pallas_call(
        _outproj_kernel,
        grid=(s // tm,),
        in_specs=[pl.BlockSpec((tm, d), lambda i: (i, 0)),
                  pl.BlockSpec((None, d, d), lambda i: (layer, 0, 0), pipeline_mode=pl.Buffered(1)),
                  pl.BlockSpec((tm, d), lambda i: (i, 0)),
                  pl.BlockSpec((1, d), lambda i: (0, 0))],
        out_specs=[pl.BlockSpec((tm, d), lambda i: (i, 0)),
                   pl.BlockSpec((tm, d), lambda i: (i, 0))],
        out_shape=[jax.ShapeDtypeStruct((s, d), F32), jax.ShapeDtypeStruct((s, d), BF16)],
        compiler_params=_params("parallel"),
        name="outproj",
    )(merged, w_out, x, g.reshape(1, d))


def _mlp_kernel(u_ref, wu_ref, wd_ref, x_ref, *rest):
    g_ref, o_ref, un_ref = rest if len(rest) == 3 else (None, rest[0], None)
    f = pl.program_id(1)

    @pl.when(f == 0)
    def _():
        o_ref[...] = x_ref[...]

    h = jnp.dot(u_ref[...], wu_ref[...], preferred_element_type=F32)
    h = jnp.square(jnp.maximum(h, 0.0)).astype(BF16)
    o_ref[...] += jnp.dot(h, wd_ref[...], preferred_element_type=F32)

    if un_ref is not None:
        @pl.when(f == pl.num_programs(1) - 1)
        def _():
            y = o_ref[...]
            ms = jnp.mean(y * y, axis=-1, keepdims=True)
            un_ref[...] = (y * lax.rsqrt(ms + EPS) * g_ref[...]).astype(un_ref.dtype)


def _mlp(u, w_up, w_down, layer, x, next_gain=None, tm=512, tf=1024):
    s, d = x.shape
    f = w_up.shape[2]
    row = pl.BlockSpec((tm, d), lambda i, j: (i, 0))
    args = [u, w_up, w_down, x]
    in_specs = [row,
                pl.BlockSpec((None, d, tf), lambda i, j: (layer, 0, j)),
                pl.BlockSpec((None, tf, d), lambda i, j: (layer, j, 0)),
                row]
    out_specs, out_shape = row, jax.ShapeDtypeStruct((s, d), F32)
    if next_gain is not None:
        args.append(next_gain.reshape(1, d))
        in_specs.append(pl.BlockSpec((1, d), lambda i, j: (0, 0)))
        out_specs, out_shape = [row, row], [out_shape, jax.ShapeDtypeStruct((s, d), BF16)]
    return pl.pallas_call(
        _mlp_kernel,
        grid=(s // tm, f // tf),
        in_specs=in_specs,
        out_specs=out_specs,
        out_shape=out_shape,
        compiler_params=_params("parallel", "arbitrary"),
        name="mlp",
    )(*args)


def _rope_tables(pos):
    half = ROPE_DIM // 2
    inv = ROPE_THETA ** (-jnp.arange(half, dtype=F32) / half)
    ang = pos.astype(F32)[:, None] * inv[None, :]
    cos, sin = jnp.cos(ang), jnp.sin(ang)
    n = pos.shape[0]
    pad = HEAD_DIM - ROPE_DIM
    t_cos = jnp.concatenate([cos, cos, jnp.ones((n, pad), F32)], axis=1)
    t_sin = jnp.concatenate([sin, sin, jnp.zeros((n, pad), F32)], axis=1)
    return tuple(jnp.tile(t, (1, LANES // HEAD_DIM)) for t in (t_cos, t_sin))


def _retention_rot_tables(pos):
    half = C_HEAD_DIM // 2
    inv = C_ROT_THETA ** (-jnp.arange(half, dtype=F32) / half)
    ang = pos.astype(F32)[:, None] * inv[None, :]
    cos, sin = jnp.cos(ang), jnp.sin(ang)
    return jnp.concatenate([cos, cos], axis=1), jnp.concatenate([-sin, sin], axis=1)


def _retention_decay_tables():
    c = RET_CHUNK
    log_g = jnp.log1p(-(2.0 ** (-5.0 - jnp.arange(C_HEADS, dtype=F32))))
    i = jnp.arange(c, dtype=F32)
    rel = i[:, None] - i[None, :]
    decay = jnp.where(rel >= 0, jnp.exp(log_g[:, None, None] * jnp.maximum(rel, 0.0)), 0.0)
    rows = lambda t: jnp.broadcast_to(t[:, :, None], (C_HEADS, c, C_HEAD_DIM))
    q_decay = rows(jnp.exp(log_g[:, None] * (i + 1.0)[None, :]))
    k_decay = rows(jnp.exp(log_g[:, None] * (c - 1 - i)[None, :]))
    chunk_decay = jnp.broadcast_to(jnp.exp(log_g * c)[:, None, None], (C_HEADS, C_HEAD_DIM, C_HEAD_DIM))
    return decay, q_decay, k_decay, chunk_decay


def _b_head_order():
    order = []
    for p in range(B_PAIRS):
        g2, a = divmod(p, B_PAIRS // 2)
        order += [B_REP * (2 * g2) + a, B_REP * (2 * g2 + 1) + a]
    return order


PREP_ROWS = 128


def _regroup_in_kernel(pieces, w_ref, *o_refs):
    for out, dst, src, width in pieces:
        o_refs[out][:, dst:dst + width] = w_ref[:, src:src + width].astype(BF16)


def _regroup_in(w_in, b_order):
    depth, k, _ = w_in.shape
    start = [sum(IN_SIZES[:i]) for i in range(len(IN_SIZES) + 1)]
    pieces = []
    for g in range(A_GROUPS):
        for t in range(3):
            pieces.append((0, (3 * g + t) * A_GROUP_WIDTH, start[t] + g * A_GROUP_WIDTH, A_GROUP_WIDTH))
    for slot, h in enumerate(b_order):
        pieces.append((1, slot * HEAD_DIM, start[3] + h * HEAD_DIM, HEAD_DIM))
    pieces.append((1, B_Q_WIDTH, start[4], 2 * B_KV_WIDTH))
    pieces.append((2, 0, start[6], start[13] - start[6]))
    widths = (3 * A_GROUPS * A_GROUP_WIDTH, B_Q_WIDTH + 2 * B_KV_WIDTH, start[13] - start[6])
    return pl.pallas_call(
        functools.partial(_regroup_in_kernel, tuple(pieces)),
        grid=(depth, k // PREP_ROWS),
        in_specs=[pl.BlockSpec((None, PREP_ROWS, start[13]), lambda l, i: (l, i, 0))],
        out_specs=[pl.BlockSpec((None, PREP_ROWS, w), lambda l, i: (l, i, 0)) for w in widths],
        out_shape=[jax.ShapeDtypeStruct((depth, k, w), BF16) for w in widths],
        compiler_params=_params("parallel", "parallel"),
        name="regroup_in",
    )(w_in)


def _permute_rows_kernel(order, w_ref, o_ref):
    for slot, h in enumerate(order):
        o_ref[slot * HEAD_DIM:(slot + 1) * HEAD_DIM, :] = w_ref[h * HEAD_DIM:(h + 1) * HEAD_DIM, :].astype(BF16)


def _permute_head_rows(w, order):
    depth, rows, n = w.shape
    return pl.pallas_call(
        functools.partial(_permute_rows_kernel, tuple(order)),
        grid=(depth,),
        in_specs=[pl.BlockSpec((None, rows, n), lambda l: (l, 0, 0))],
        out_specs=pl.BlockSpec((None, rows, n), lambda l: (l, 0, 0)),
        out_shape=jax.ShapeDtypeStruct((depth, rows, n), BF16),
        compiler_params=_params("parallel"),
        name="permute_rows",
    )(w)


def kernel(x, mix_norm, w_in, a_q_norm, a_k_norm, b_q_norm, b_k_norm, b_sinks, c_gn,
           w_br_a, w_br_b, w_br_c, w_out, mlp_norm, w_up, w_down):
    b, s, d = x.shape
    assert b == 1 and d == D_MODEL and s % A_SUPER == 0
    depth = w_in.shape[0]
    x = x.reshape(s, d)
    pos = jnp.arange(s)
    rope = _rope_tables(pos)
    rot_c = _retention_rot_tables(pos)
    decay_tabs = _retention_decay_tables()
    width2 = 2 * LANES
    bd = (jnp.arange(width2)[:, None] // HEAD_DIM == jnp.arange(width2)[None, :] // HEAD_DIM).astype(BF16)
    b_order = _b_head_order()
    q_scale = HEAD_DIM ** -0.5 * LOG2E
    tile = jnp.tile

    w_a, w_b, w_tail = _regroup_in(w_in, b_order)
    tail_col = lambda seg: sum(IN_SIZES[6:seg])
    wa = w_br_a.astype(BF16)
    wb = _permute_head_rows(w_br_b, b_order)
    wc = w_br_c.astype(BF16)
    wo = w_out.astype(BF16)
    wu = w_up.astype(BF16)
    wd = w_down.astype(BF16)
    a_width = 3 * A_GROUP_WIDTH
    b_width = B_Q_WIDTH + 2 * B_KV_WIDTH

    vec_a = tile(jnp.concatenate([tile(a_q_norm * q_scale, (1, A_HEADS_PER_GROUP)),
                                  tile(a_k_norm, (1, A_HEADS_PER_GROUP)),
                                  jnp.ones((depth, A_GROUP_WIDTH), F32)], axis=1), (1, A_GROUPS))
    vec_b = jnp.concatenate([tile(b_q_norm * q_scale, (1, B_Q_HEADS)), tile(b_k_norm, (1, B_KV_HEADS)),
                             jnp.ones((depth, B_KV_WIDTH), F32)], axis=1)
    vec_c = jnp.concatenate([jnp.ones((C_WIDTH,), F32), jnp.full((C_WIDTH,), C_HEAD_DIM ** -0.5, F32),
                             jnp.ones((2 * C_WIDTH,), F32)])
    half = B_PAIRS // 2
    sinks = (b_sinks.astype(F32) * LOG2E).reshape(depth, 2, 2, half).transpose(0, 1, 3, 2).reshape(depth, B_Q_HEADS)
    assert np.arange(B_Q_HEADS).reshape(2, 2, half).transpose(0, 2, 1).reshape(-1).tolist() == b_order

    u = _rmsnorm(x, mix_norm[0])
    for l in range(depth):
        kinds = ["qk"] * (2 * A_PAIRS) + ["plain"] * A_PAIRS
        qkv = _proj(u, w_a, l, 0, [(kinds, dil) for _, dil in A_DILATED], tm=A_SUPER, tn=a_width,
                    vec=vec_a[l], tabs=rope, bd=bd)
        o_a = _attn_a(qkv)

        kinds = ["qk"] * ((B_Q_WIDTH + B_KV_WIDTH) // LANES) + ["plain"] * (B_KV_WIDTH // LANES)
        bqkv = _proj(u, w_b, l, 0, [(kinds, 1)], tm=1024, tn=b_width, vec=vec_b[l], tabs=rope, bd=bd)
        o_b = _attn_b(bqkv, sinks[l])

        heads = C_WIDTH // LANES
        c_all = _proj(u, w_tail, l, tail_col(6), [(["rot"] * (2 * heads), 1), (["plain"] * heads + ["silu"] * heads, 1)],
                      tm=1024, tn=2 * C_WIDTH, vec=vec_c, tabs=rot_c)
        o_c = _retention(c_all, c_gn[l], decay_tabs)

        merged = _branch(u, w_tail, [tail_col(seg) for seg in (10, 11, 12)], o_a, o_b, o_c, wa, wb, wc, l)
        x, u2 = _outproj(merged, wo, l, x, mlp_norm[l])
        if l + 1 < depth:
            x, u = _mlp(u2, wu, wd, l, x, next_gain=mix_norm[l + 1])
        else:
            x = _mlp(u2, wu, wd, l, x)
    return x.reshape(b, s, d)
```

```python
import functools
import math

import jax
import jax.numpy as jnp
import numpy as np
from jax import lax
from jax.experimental import pallas as pl
from jax.experimental.pallas import tpu as pltpu

F32 = jnp.float32
BF16 = jnp.bfloat16

D_MODEL = 2048
EPS = 1e-6
NEG_INF = -1e30
BLOCK = 128
LANES = 128
HEAD_DIM = 64
ROPE_DIM = HEAD_DIM // 4
ROPE_THETA = 500000.0
A_DILATED = ((128, 1), (512, 4), (2048, 16))
A_GROUPS = len(A_DILATED)
A_HEADS_PER_GROUP = 6
A_GROUP_WIDTH = A_HEADS_PER_GROUP * HEAD_DIM
A_PAIRS = A_GROUP_WIDTH // LANES
A_SUPER = BLOCK * max(d for _, d in A_DILATED)
B_WINDOW = 128
B_Q_HEADS = 16
B_KV_HEADS = 4
B_REP = B_Q_HEADS // B_KV_HEADS
B_Q_WIDTH = B_Q_HEADS * HEAD_DIM
B_KV_WIDTH = B_KV_HEADS * HEAD_DIM
B_PAIRS = B_Q_WIDTH // LANES
C_HEADS = 8
C_HEAD_DIM = 128
C_WIDTH = C_HEADS * C_HEAD_DIM
C_ROT_THETA = 10000.0
RET_CHUNK = 256
RET_UNROLL = 8
D_FF = 4 * D_MODEL
IN_SIZES = (A_GROUPS * A_GROUP_WIDTH,) * 3 + (B_Q_WIDTH, B_KV_WIDTH, B_KV_WIDTH) + (C_WIDTH,) * 4 + (D_MODEL,) * 3

VMEM_LIMIT = 56 * 1024 * 1024
LOOP_UNROLL = 16
LOG2E = math.log2(math.e)
PROJ_UNROLL = 2
PROJ_SUB = 512


def _params(*sem):
    return pltpu.CompilerParams(dimension_semantics=sem, vmem_limit_bytes=VMEM_LIMIT)


def _sigmoid(a):
    return 1.0 / (1.0 + jnp.exp(-a))


def _rmsnorm_kernel(x_ref, g_ref, o_ref):
    x = x_ref[...]
    ms = jnp.mean(x * x, axis=-1, keepdims=True)
    o_ref[...] = (x * lax.rsqrt(ms + EPS) * g_ref[...]).astype(o_ref.dtype)


def _rmsnorm(x, g, tm=512):
    s, d = x.shape
    return pl.pallas_call(
        _rmsnorm_kernel,
        grid=(s // tm,),
        in_specs=[pl.BlockSpec((tm, d), lambda i: (i, 0)),
                  pl.BlockSpec((1, d), lambda i: (0, 0))],
        out_specs=pl.BlockSpec((tm, d), lambda i: (i, 0)),
        out_shape=jax.ShapeDtypeStruct((s, d), BF16),
        compiler_params=_params("parallel"),
        name="rmsnorm",
    )(x, g.reshape(1, d))


PERM_STAGE = 4


def _store_residue_major(o_ref, scr_a, scr_b, y, c, sl, row0, perm):
    sub = y.shape[0]
    chunk_rows = BLOCK * perm
    base = row0 // chunk_rows * chunk_rows + row0 % chunk_rows // perm
    n = sub // perm
    if isinstance(base, int):
        dst = lambda r: slice(base + r * BLOCK, base + r * BLOCK + n)
    else:
        dst = lambda r: pl.ds(pl.multiple_of(base + r * BLOCK, n), n)
    scr_a[c] = y
    if perm <= PERM_STAGE:
        for r in range(perm):
            o_ref[dst(r), sl] = scr_a[c, pl.ds(r, n, stride=perm), :].astype(o_ref.dtype)
        return
    outer = perm // PERM_STAGE
    assert outer <= PERM_STAGE and perm % PERM_STAGE == 0
    part = sub // PERM_STAGE
    for r0 in range(PERM_STAGE):
        scr_b[c, r0 * part:(r0 + 1) * part, :] = scr_a[c, pl.ds(r0, part, stride=PERM_STAGE), :]
    for r0 in range(PERM_STAGE):
        for r1 in range(outer):
            o_ref[dst(r1 * PERM_STAGE + r0), sl] = (
                scr_b[c, pl.ds(r0 * part + r1, n, stride=outer), :].astype(o_ref.dtype))


def _proj_kernel(tiles, sub, tile_axis, *refs):
    refs = list(refs)
    any_perm = any(p > 1 for _, p in tiles)
    scr_b = refs.pop() if any_perm else None
    scr_a = refs.pop() if any_perm else None
    o_ref = refs.pop()
    u_ref, w_ref = refs[0], refs[1]
    rest = refs[2:]
    all_kinds = [k for kinds, _ in tiles for k in kinds]
    need_tab = any(k in ("qk", "rot") for k in all_kinds)
    vec_ref = rest.pop(0) if need_tab else None
    cos_ref, sin_ref = (rest.pop(0), rest.pop(0)) if need_tab else (None, None)
    bd_ref = rest.pop(0) if "qk" in all_kinds else None
    dim = lax.broadcasted_iota(jnp.int32, (sub, LANES), 1) & (HEAD_DIM - 1)
    half = ROPE_DIM // 2

    def tile_body(kinds, perm):
        n_qk = sum(k == "qk" for k in kinds)
        assert n_qk % 2 == 0 and all(k == "qk" for k in kinds[:n_qk])
        n_sub = u_ref.shape[0] // sub

        def sub_tile(m):
            row0 = m * sub if isinstance(m, int) else pl.multiple_of(m * sub, sub)
            rows = pl.ds(row0, sub)
            acc = jnp.dot(u_ref[rows, :], w_ref[...], preferred_element_type=F32)
            ss = []
            for c in range(n_qk // 2):
                a2 = acc[:, 2 * c * LANES:2 * (c + 1) * LANES]
                s2 = jnp.dot((a2 * a2).astype(BF16), bd_ref[...], preferred_element_type=F32)
                ss += [s2[:, :LANES], s2[:, LANES:]]
            for c, kind in enumerate(kinds):
                sl = slice(c * LANES, (c + 1) * LANES)
                a = acc[:, sl]
                if kind == "qk":
                    y = a * lax.rsqrt(ss[c] * (1.0 / HEAD_DIM) + EPS) * vec_ref[:, sl]
                    turned = jnp.where(dim < half, -pltpu.roll(y, LANES - half, 1),
                                       jnp.where(dim < ROPE_DIM, pltpu.roll(y, half, 1), 0.0))
                    y = y * cos_ref[rows, :] + turned * sin_ref[rows, :]
                elif kind == "rot":
                    y = (a * cos_ref[rows, :] + pltpu.roll(a, LANES // 2, 1) * sin_ref[rows, :]) * vec_ref[:, sl]
                elif kind == "sigmoid":
                    y = _sigmoid(a)
                elif kind == "silu":
                    y = a * _sigmoid(a)
                elif kind == "relu2":
                    y = jnp.square(jnp.maximum(a, 0.0))
                else:
                    y = a
                if perm == 1:
                    o_ref[rows, sl] = y.astype(o_ref.dtype)
                else:
                    _store_residue_major(o_ref, scr_a, scr_b, y, c, sl, row0, perm)

        if n_sub <= PROJ_UNROLL:
            for m in range(n_sub):
                sub_tile(m)
        else:
            def pair(t, carry):
                for k in range(PROJ_UNROLL):
                    sub_tile(t * PROJ_UNROLL + k)
                return carry
            lax.fori_loop(0, n_sub // PROJ_UNROLL, pair, 0)

    if len(tiles) == 1:
        tile_body(*tiles[0])
    else:
        for j, (kinds, perm) in enumerate(tiles):
            pl.when(pl.program_id(tile_axis) == j)(functools.partial(tile_body, kinds, perm))


def _proj(u, w, layer, col0, tiles, tm, tn, vec=None, tabs=None, bd=None, sub=PROJ_SUB, weight_stationary=False):
    s, k = u.shape
    tiles = tuple((tuple(kinds), perm) for kinds, perm in tiles)
    n = len(tiles) * tn
    assert col0 % tn == 0 and s % tm == 0 and tm % sub == 0
    for kinds, perm in tiles:
        assert len(kinds) == tn // LANES
        assert perm == 1 or (sub % perm == 0 and (BLOCK * perm) % sub == 0 and tm % (BLOCK * perm) == 0)
    if all(t == tiles[0] for t in tiles):
        tiles = tiles[:1]
    j0 = col0 // tn
    at = (lambda f: (lambda j, i: f(i, j))) if weight_stationary else (lambda f: f)
    grid = (n // tn, s // tm) if weight_stationary else (s // tm, n // tn)
    args = [u, w]
    in_specs = [pl.BlockSpec((tm, k), at(lambda i, j: (i, 0))),
                pl.BlockSpec((None, k, tn), at(lambda i, j: (layer, 0, j0 + j)))]
    if tabs is not None:
        args.append(vec.reshape(1, n).astype(F32))
        in_specs.append(pl.BlockSpec((1, tn), at(lambda i, j: (0, j))))
        for t in tabs:
            args.append(t)
            in_specs.append(pl.BlockSpec((tm, LANES), at(lambda i, j: (i, 0))))
    if bd is not None:
        args.append(bd)
        in_specs.append(pl.BlockSpec(bd.shape, at(lambda i, j: (0, 0))))
    scratch = []
    if any(p > 1 for _, p in tiles):
        scratch = [pltpu.VMEM((tn // LANES, sub, LANES), F32)] * 2
    return pl.pallas_call(
        functools.partial(_proj_kernel, tiles, sub, 0 if weight_stationary else 1),
        grid=grid,
        in_specs=in_specs,
        out_specs=pl.BlockSpec((tm, tn), at(lambda i, j: (i, j))),
        out_shape=jax.ShapeDtypeStruct((s, n), BF16),
        scratch_shapes=scratch,
        compiler_params=_params("parallel", "arbitrary"),
        name="proj_" + tiles[0][0][0],
    )(*args)


def _half_masks(rows):
    lane = lax.broadcasted_iota(jnp.int32, (rows, LANES), 1)
    lo = jnp.where(lane < HEAD_DIM, 1.0, 0.0)
    return lo.astype(BF16), (1.0 - lo).astype(BF16)


def _pair_attend(q, k2, v2, bias, sink_key=False):
    if sink_key:
        rows = 16
        keep = jnp.where(lax.broadcasted_iota(jnp.int32, (rows, LANES), 0) > 0, 1.0, 0.0).astype(BF16)
        k2 = jnp.concatenate([k2[:rows] * keep, k2[rows:]], axis=0)
        v2 = jnp.concatenate([v2[:rows] * keep, v2[rows:]], axis=0)
    lo_b, hi_b = _half_masks(BLOCK)
    q2 = jnp.concatenate([q * lo_b, q * hi_b], axis=0)
    s2 = lax.dot_general(q2, k2, (((1,), (1,)), ((), ())), preferred_element_type=F32) + bias
    m2 = jnp.max(s2, axis=-1, keepdims=True)
    p2 = jnp.exp2(s2 - m2).astype(BF16)
    vx = jnp.concatenate([v2, jnp.ones_like(v2)], axis=1)
    r2 = jnp.dot(p2, vx, preferred_element_type=F32)
    lo = lax.broadcasted_iota(jnp.int32, (BLOCK, LANES), 1) < HEAD_DIM
    num = jnp.where(lo, r2[:BLOCK, :LANES], r2[BLOCK:, :LANES])
    den = jnp.where(lo, r2[:BLOCK, LANES:], r2[BLOCK:, LANES:])
    mx = jnp.where(lo, m2[:BLOCK], m2[BLOCK:])
    return num, den, mx


def _fill_bias(bias_ref, max_dist, sinks=None):
    row = lax.broadcasted_iota(jnp.int32, (2 * BLOCK, 2 * BLOCK), 0)
    kj = lax.broadcasted_iota(jnp.int32, (2 * BLOCK, 2 * BLOCK), 1)
    dist = (row & (BLOCK - 1)) + BLOCK - kj
    band = jnp.where(dist >= 0, jnp.where(dist <= max_dist, 0.0, NEG_INF), NEG_INF)
    first = jnp.where(kj >= BLOCK, band, NEG_INF)
    if sinks is not None:
        assert max_dist < BLOCK
        sink = jnp.where(row < BLOCK, sinks[0], sinks[1])
        band = jnp.where(kj == 0, sink, band)
        first = jnp.where(kj == 0, sink, first)
    bias_ref[0] = band
    bias_ref[1] = first


def _fill_ext(ext_ref, prev_ref, cur_ref):
    p = prev_ref.shape[0]
    ext_ref[0:p, :] = prev_ref[...]
    ext_ref[p:, :] = cur_ref[...]


def _attn_a_kernel(*refs):
    ins, (o_ref,), scr = refs[:15], refs[15:16], refs[16:]
    sb = pl.program_id(0)
    o_scr, l_scr, bias_ref = scr[0], scr[1], scr[2]
    _fill_bias(bias_ref, BLOCK)
    for g, (_, dil) in enumerate(A_DILATED):
        q_ref, k_ref, v_ref, kp_ref, vp_ref = ins[5 * g:5 * g + 5]
        kx_ref, vx_ref = scr[3 + 2 * g], scr[4 + 2 * g]
        _fill_ext(kx_ref, kp_ref, k_ref)
        _fill_ext(vx_ref, vp_ref, v_ref)
        prev_rows = BLOCK * dil

        def body(b, carry, g=g, dil=dil, q_ref=q_ref, kx_ref=kx_ref, vx_ref=vx_ref, prev_rows=prev_rows):
            row = pl.multiple_of(b * BLOCK, BLOCK)
            q = q_ref[pl.ds(row, BLOCK), :]
            k2 = jnp.concatenate([kx_ref[pl.ds(row, BLOCK), :],
                                  kx_ref[pl.ds(row + prev_rows, BLOCK), :]], axis=0)
            v2 = jnp.concatenate([vx_ref[pl.ds(row, BLOCK), :],
                                  vx_ref[pl.ds(row + prev_rows, BLOCK), :]], axis=0)
            no_prev = jnp.logical_and(sb == 0, b < dil)
            num, den, mx = _pair_attend(q, k2, v2, bias_ref[no_prev.astype(jnp.int32)])
            if dil == 1:
                dst = pl.ds(row, BLOCK)
            else:
                chunk = b // dil
                res = b - chunk * dil
                dst = pl.ds(chunk * prev_rows + res, BLOCK, stride=dil)
            o_scr[g, dst, :] = num / den
            l_scr[g, dst, :] = mx + jnp.log2(den)
            return carry

        lax.fori_loop(0, A_SUPER // BLOCK, body, 0, unroll=LOOP_UNROLL)

    l0, l1, l2 = l_scr[0], l_scr[1], l_scr[2]
    lm = jnp.maximum(jnp.maximum(l0, l1), l2)
    w0, w1, w2 = jnp.exp2(l0 - lm), jnp.exp2(l1 - lm), jnp.exp2(l2 - lm)
    o = (w0 * o_scr[0] + w1 * o_scr[1] + w2 * o_scr[2]) / (w0 + w1 + w2)
    o_ref[...] = o.astype(o_ref.dtype)


def _attn_a(qkv):
    s = qkv.shape[0]
    args, in_specs = [], []
    scratch = [pltpu.VMEM((A_GROUPS, A_SUPER, LANES), F32),
               pltpu.VMEM((A_GROUPS, A_SUPER, LANES), F32),
               pltpu.VMEM((2, 2 * BLOCK, 2 * BLOCK), F32)]
    for g, (_, dil) in enumerate(A_DILATED):
        prev_rows = BLOCK * dil
        per = A_SUPER // prev_rows
        col = 3 * A_PAIRS * g

        def cur_map(off):
            return lambda i, p: (i, off + p)

        def prev_map(off, per=per):
            return lambda i, p: (jnp.maximum(i * per - 1, 0), off + p)

        args += [qkv] * 5
        in_specs += [pl.BlockSpec((A_SUPER, LANES), cur_map(col)),
                     pl.BlockSpec((A_SUPER, LANES), cur_map(col + A_PAIRS)),
                     pl.BlockSpec((A_SUPER, LANES), cur_map(col + 2 * A_PAIRS)),
                     pl.BlockSpec((prev_rows, LANES), prev_map(col + A_PAIRS)),
                     pl.BlockSpec((prev_rows, LANES), prev_map(col + 2 * A_PAIRS))]
        scratch += [pltpu.VMEM((prev_rows + A_SUPER, LANES), BF16)] * 2
    return pl.pallas_call(
        _attn_a_kernel,
        grid=(s // A_SUPER, A_PAIRS),
        in_specs=in_specs,
        out_specs=pl.BlockSpec((A_SUPER, LANES), lambda i, p: (i, p)),
        out_shape=jax.ShapeDtypeStruct((s, A_GROUP_WIDTH), BF16),
        scratch_shapes=scratch,
        compiler_params=_params("parallel", "parallel"),
        name="attn_a",
    )(*args)


def _attn_b_kernel(q_ref, k_ref, v_ref, kp_ref, vp_ref, sink_ref, o_ref, kx_ref, vx_ref, bias_ref):
    i = pl.program_id(0)
    p = pl.program_id(1)
    _fill_bias(bias_ref, B_WINDOW - 1, sinks=(sink_ref[2 * p], sink_ref[2 * p + 1]))
    _fill_ext(kx_ref, kp_ref, k_ref)
    _fill_ext(vx_ref, vp_ref, v_ref)

    def body(b, carry):
        row = pl.multiple_of(b * BLOCK, BLOCK)
        q = q_ref[pl.ds(row, BLOCK), :]
        k2 = kx_ref[pl.ds(row, 2 * BLOCK), :]
        v2 = vx_ref[pl.ds(row, 2 * BLOCK), :]
        no_prev = jnp.logical_and(i == 0, b == 0)
        num, den, _ = _pair_attend(q, k2, v2, bias_ref[no_prev.astype(jnp.int32)], sink_key=True)
        o_ref[pl.ds(row, BLOCK), :] = (num / den).astype(o_ref.dtype)
        return carry

    lax.fori_loop(0, q_ref.shape[0] // BLOCK, body, 0, unroll=LOOP_UNROLL)


def _attn_b(qkv, sinks, tq=2048):
    s = qkv.shape[0]
    per = tq // BLOCK
    kv = lambda p: p // (B_PAIRS // 2)
    k0 = B_PAIRS
    v0 = B_PAIRS + B_KV_WIDTH // LANES
    return pl.pallas_call(
        _attn_b_kernel,
        grid=(s // tq, B_PAIRS),
        in_specs=[pl.BlockSpec((tq, LANES), lambda i, p: (i, p)),
                  pl.BlockSpec((tq, LANES), lambda i, p: (i, k0 + kv(p))),
                  pl.BlockSpec((tq, LANES), lambda i, p: (i, v0 + kv(p))),
                  pl.BlockSpec((BLOCK, LANES), lambda i, p: (jnp.maximum(i * per - 1, 0), k0 + kv(p))),
                  pl.BlockSpec((BLOCK, LANES), lambda i, p: (jnp.maximum(i * per - 1, 0), v0 + kv(p))),
                  pl.BlockSpec(memory_space=pltpu.SMEM)],
        out_specs=pl.BlockSpec((tq, LANES), lambda i, p: (i, p)),
        out_shape=jax.ShapeDtypeStruct((s, B_Q_WIDTH), BF16),
        scratch_shapes=[pltpu.VMEM((BLOCK + tq, LANES), BF16)] * 2
                       + [pltpu.VMEM((2, 2 * BLOCK, 2 * BLOCK), F32)],
        compiler_params=_params("parallel", "parallel"),
        name="attn_b",
    )(qkv, qkv, qkv, qkv, qkv, sinks)


def _retention_kernel(q_ref, k_ref, v_ref, g_ref, gn_ref, decay_ref, qd_ref, kd_ref, cd_ref, o_ref, state_ref):
    i = pl.program_id(0)
    h = pl.program_id(1)

    @pl.when(i == 0)
    def _():
        state_ref[h] = jnp.zeros((C_HEAD_DIM, C_HEAD_DIM), F32)

    gn = gn_ref[...]

    def body(n, state):
        row = pl.multiple_of(n * RET_CHUNK, RET_CHUNK)
        q = q_ref[pl.ds(row, RET_CHUNK), :]
        k = k_ref[pl.ds(row, RET_CHUNK), :]
        v = v_ref[pl.ds(row, RET_CHUNK), :]
        s = lax.dot_general(q, k, (((1,), (1,)), ((), ())), preferred_element_type=F32) * decay_ref[0]
        inner = jnp.dot(s.astype(BF16), v, preferred_element_type=F32)
        cross = jnp.dot(q, state.astype(BF16), preferred_element_type=F32) * qd_ref[0]
        kd_t = (k.astype(F32) * kd_ref[0]).T.astype(BF16)
        new_state = cd_ref[0] * state + jnp.dot(kd_t, v, preferred_element_type=F32)
        y = inner + cross
        y = y * lax.rsqrt(jnp.mean(y * y, axis=-1, keepdims=True) + EPS)
        o_ref[pl.ds(row, RET_CHUNK), :] = (g_ref[pl.ds(row, RET_CHUNK), :].astype(F32) * (y * gn)).astype(o_ref.dtype)
        return new_state

    state_ref[h] = lax.fori_loop(0, q_ref.shape[0] // RET_CHUNK, body, state_ref[h], unroll=RET_UNROLL)


def _retention(c_all, c_gn, tabs, tq=2048):
    s = c_all.shape[0]
    v_off = 2 * C_HEADS
    g_off = 3 * C_HEADS
    tab = lambda t: pl.BlockSpec((1,) + t.shape[1:], lambda i, h: (h, 0, 0))
    return pl.pallas_call(
        _retention_kernel,
        grid=(s // tq, C_HEADS),
        in_specs=[pl.BlockSpec((tq, LANES), lambda i, h: (i, h)),
                  pl.BlockSpec((tq, LANES), lambda i, h: (i, C_HEADS + h)),
                  pl.BlockSpec((tq, LANES), lambda i, h: (i, v_off + h)),
                  pl.BlockSpec((tq, LANES), lambda i, h: (i, g_off + h)),
                  pl.BlockSpec((1, LANES), lambda i, h: (0, h))] + [tab(t) for t in tabs],
        out_specs=pl.BlockSpec((tq, LANES), lambda i, h: (i, h)),
        out_shape=jax.ShapeDtypeStruct((s, C_WIDTH), BF16),
        scratch_shapes=[pltpu.VMEM((C_HEADS, C_HEAD_DIM, C_HEAD_DIM), F32)],
        compiler_params=_params("arbitrary", "arbitrary"),
        name="retention",
    )(c_all, c_all, c_all, c_all, c_gn.reshape(1, C_WIDTH).astype(F32), *tabs)


def _branch_kernel(sub, u_ref, ga_ref, gb_ref, gc_ref, oa_ref, ob_ref, oc_ref, wa_ref, wb_ref, wc_ref, o_ref):
    for m in range(u_ref.shape[0] // sub):
        rows = slice(m * sub, (m + 1) * sub)
        u = u_ref[rows, :]
        total = None
        for g_ref, o_in, w_ref in ((ga_ref, oa_ref, wa_ref), (gb_ref, ob_ref, wb_ref), (gc_ref, oc_ref, wc_ref)):
            gate = _sigmoid(jnp.dot(u, g_ref[...], preferred_element_type=F32))
            term = gate * jnp.dot(o_in[rows, :], w_ref[...], preferred_element_type=F32)
            total = term if total is None else total + term
        o_ref[rows, :] = total.astype(o_ref.dtype)


def _branch(u, w_in_pad, gate_cols, o_a, o_b, o_c, w_a, w_b, w_c, layer, tm=1024, tn=512, sub=PROJ_SUB):
    s, k = u.shape
    n = w_a.shape[2]
    act = lambda width: pl.BlockSpec((tm, width), lambda i, j: (i, 0))
    wgt = lambda width: pl.BlockSpec((None, width, tn), lambda i, j: (layer, 0, j))
    assert all(c % tn == 0 for c in gate_cols)
    gate = lambda col: pl.BlockSpec((None, k, tn), lambda i, j: (layer, 0, col // tn + j))
    return pl.pallas_call(
        functools.partial(_branch_kernel, sub),
        grid=(s // tm, n // tn),
        in_specs=[act(k)] + [gate(c) for c in gate_cols]
                 + [act(o_a.shape[1]), act(o_b.shape[1]), act(o_c.shape[1]),
                    wgt(w_a.shape[1]), wgt(w_b.shape[1]), wgt(w_c.shape[1])],
        out_specs=pl.BlockSpec((tm, tn), lambda i, j: (i, j)),
        out_shape=jax.ShapeDtypeStruct((s, n), BF16),
        compiler_params=_params("parallel", "arbitrary"),
        name="branch",
    )(u, w_in_pad, w_in_pad, w_in_pad, o_a, o_b, o_c, w_a, w_b, w_c)


def _outproj_kernel(m_ref, w_ref, x_ref, g_ref, x1_ref, u_ref):
    x1 = x_ref[...] + jnp.dot(m_ref[...], w_ref[...], preferred_element_type=F32)
    x1_ref[...] = x1
    ms = jnp.mean(x1 * x1, axis=-1, keepdims=True)
    u_ref[...] = (x1 * lax.rsqrt(ms + EPS) * g_ref[...]).astype(u_ref.dtype)


def _outproj(merged, w_out, layer, x, g, tm=512):
    s, d = x.shape
    return pl.pallas_call(
        _outproj_kernel,
        grid=(s // tm,),
        in_specs=[pl.BlockSpec((tm, d), lambda i: (i, 0)),
                  pl.BlockSpec((None, d, d), lambda i: (layer, 0, 0), pipeline_mode=pl.Buffered(1)),
                  pl.BlockSpec((tm, d), lambda i: (i, 0)),
                  pl.BlockSpec((1, d), lambda i: (0, 0))],
        out_specs=[pl.BlockSpec((tm, d), lambda i: (i, 0)),
                   pl.BlockSpec((tm, d), lambda i: (i, 0))],
        out_shape=[jax.ShapeDtypeStruct((s, d), F32), jax.ShapeDtypeStruct((s, d), BF16)],
        compiler_params=_params("parallel"),
        name="outproj",
    )(merged, w_out, x, g.reshape(1, d))


def _mlp_kernel(u_ref, wu_ref, wd_ref, x_ref, *rest):
    g_ref, o_ref, un_ref = rest if len(rest) == 3 else (None, rest[0], None)
    f = pl.program_id(1)

    @pl.when(f == 0)
    def _():
        o_ref[...] = x_ref[...]

    h = jnp.dot(u_ref[...], wu_ref[...], preferred_element_type=F32)
    h = jnp.square(jnp.maximum(h, 0.0)).astype(BF16)
    o_ref[...] += jnp.dot(h, wd_ref[...], preferred_element_type=F32)

    if un_ref is not None:
        @pl.when(f == pl.num_programs(1) - 1)
        def _():
            y = o_ref[...]
            ms = jnp.mean(y * y, axis=-1, keepdims=True)
            un_ref[...] = (y * lax.rsqrt(ms + EPS) * g_ref[...]).astype(un_ref.dtype)


def _mlp(u, w_up, w_down, layer, x, next_gain=None, tm=512, tf=1024):
    s, d = x.shape
    f = w_up.shape[2]
    row = pl.BlockSpec((tm, d), lambda i, j: (i, 0))
    args = [u, w_up, w_down, x]
    in_specs = [row,
                pl.BlockSpec((None, d, tf), lambda i, j: (layer, 0, j)),
                pl.BlockSpec((None, tf, d), lambda i, j: (layer, j, 0)),
                row]
    out_specs, out_shape = row, jax.ShapeDtypeStruct((s, d), F32)
    if next_gain is not None:
        args.append(next_gain.reshape(1, d))
        in_specs.append(pl.BlockSpec((1, d), lambda i, j: (0, 0)))
        out_specs, out_shape = [row, row], [out_shape, jax.ShapeDtypeStruct((s, d), BF16)]
    return pl.pallas_call(
        _mlp_kernel,
        grid=(s // tm, f // tf),
        in_specs=in_specs,
        out_specs=out_specs,
        out_shape=out_shape,
        compiler_params=_params("parallel", "arbitrary"),
        name="mlp",
    )(*args)


def _rope_tables(pos):
    half = ROPE_DIM // 2
    inv = ROPE_THETA ** (-jnp.arange(half, dtype=F32) / half)
    ang = pos.astype(F32)[:, None] * inv[None, :]
    cos, sin = jnp.cos(ang), jnp.sin(ang)
    n = pos.shape[0]
    pad = HEAD_DIM - ROPE_DIM
    t_cos = jnp.concatenate([cos, cos, jnp.ones((n, pad), F32)], axis=1)
    t_sin = jnp.concatenate([sin, sin, jnp.zeros((n, pad), F32)], axis=1)
    return tuple(jnp.tile(t, (1, LANES // HEAD_DIM)) for t in (t_cos, t_sin))


def _retention_rot_tables(pos):
    half = C_HEAD_DIM // 2
    inv = C_ROT_THETA ** (-jnp.arange(half, dtype=F32) / half)
    ang = pos.astype(F32)[:, None] * inv[None, :]
    cos, sin = jnp.cos(ang), jnp.sin(ang)
    return jnp.concatenate([cos, cos], axis=1), jnp.concatenate([-sin, sin], axis=1)


def _retention_decay_tables():
    c = RET_CHUNK
    log_g = jnp.log1p(-(2.0 ** (-5.0 - jnp.arange(C_HEADS, dtype=F32))))
    i = jnp.arange(c, dtype=F32)
    rel = i[:, None] - i[None, :]
    decay = jnp.where(rel >= 0, jnp.exp(log_g[:, None, None] * jnp.maximum(rel, 0.0)), 0.0)
    rows = lambda t: jnp.broadcast_to(t[:, :, None], (C_HEADS, c, C_HEAD_DIM))
    q_decay = rows(jnp.exp(log_g[:, None] * (i + 1.0)[None, :]))
    k_decay = rows(jnp.exp(log_g[:, None] * (c - 1 - i)[None, :]))
    chunk_decay = jnp.broadcast_to(jnp.exp(log_g * c)[:, None, None], (C_HEADS, C_HEAD_DIM, C_HEAD_DIM))
    return decay, q_decay, k_decay, chunk_decay


def _b_head_order():
    order = []
    for p in range(B_PAIRS):
        g2, a = divmod(p, B_PAIRS // 2)
        order += [B_REP * (2 * g2) + a, B_REP * (2 * g2 + 1) + a]
    return order


PREP_ROWS = 128


def _regroup_in_kernel(pieces, w_ref, *o_refs):
    for out, dst, src, width in pieces:
        o_refs[out][:, dst:dst + width] = w_ref[:, src:src + width].astype(BF16)


def _regroup_in(w_in, b_order):
    depth, k, _ = w_in.shape
    start = [sum(IN_SIZES[:i]) for i in range(len(IN_SIZES) + 1)]
    pieces = []
    for g in range(A_GROUPS):
        for t in range(3):
            pieces.append((0, (3 * g + t) * A_GROUP_WIDTH, start[t] + g * A_GROUP_WIDTH, A_GROUP_WIDTH))
    for slot, h in enumerate(b_order):
        pieces.append((1, slot * HEAD_DIM, start[3] + h * HEAD_DIM, HEAD_DIM))
    pieces.append((1, B_Q_WIDTH, start[4], 2 * B_KV_WIDTH))
    pieces.append((2, 0, start[6], start[13] - start[6]))
    widths = (3 * A_GROUPS * A_GROUP_WIDTH, B_Q_WIDTH + 2 * B_KV_WIDTH, start[13] - start[6])
    return pl.pallas_call(
        functools.partial(_regroup_in_kernel, tuple(pieces)),
        grid=(depth, k // PREP_ROWS),
        in_specs=[pl.BlockSpec((None, PREP_ROWS, start[13]), lambda l, i: (l, i, 0))],
        out_specs=[pl.BlockSpec((None, PREP_ROWS, w), lambda l, i: (l, i, 0)) for w in widths],
        out_shape=[jax.ShapeDtypeStruct((depth, k, w), BF16) for w in widths],
        compiler_params=_params("parallel", "parallel"),
        name="regroup_in",
    )(w_in)


def _permute_rows_kernel(order, w_ref, o_ref):
    for slot, h in enumerate(order):
        o_ref[slot * HEAD_DIM:(slot + 1) * HEAD_DIM, :] = w_ref[h * HEAD_DIM:(h + 1) * HEAD_DIM, :].astype(BF16)


def _permute_head_rows(w, order):
    depth, rows, n = w.shape
    return pl.pallas_call(
        functools.partial(_permute_rows_kernel, tuple(order)),
        grid=(depth,),
        in_specs=[pl.BlockSpec((None, rows, n), lambda l: (l, 0, 0))],
        out_specs=pl.BlockSpec((None, rows, n), lambda l: (l, 0, 0)),
        out_shape=jax.ShapeDtypeStruct((depth, rows, n), BF16),
        compiler_params=_params("parallel"),
        name="permute_rows",
    )(w)


def kernel(x, mix_norm, w_in, a_q_norm, a_k_norm, b_q_norm, b_k_norm, b_sinks, c_gn,
           w_br_a, w_br_b, w_br_c, w_out, mlp_norm, w_up, w_down):
    b, s, d = x.shape
    assert b == 1 and d == D_MODEL and s % A_SUPER == 0
    depth = w_in.shape[0]
    x = x.reshape(s, d)
    pos = jnp.arange(s)
    rope = _rope_tables(pos)
    rot_c = _retention_rot_tables(pos)
    decay_tabs = _retention_decay_tables()
    width2 = 2 * LANES
    bd = (jnp.arange(width2)[:, None] // HEAD_DIM == jnp.arange(width2)[None, :] // HEAD_DIM).astype(BF16)
    b_order = _b_head_order()
    q_scale = HEAD_DIM ** -0.5 * LOG2E
    tile = jnp.tile

    w_a, w_b, w_tail = _regroup_in(w_in, b_order)
    tail_col = lambda seg: sum(IN_SIZES[6:seg])
    wa = w_br_a.astype(BF16)
    wb = _permute_head_rows(w_br_b, b_order)
    wc = w_br_c.astype(BF16)
    wo = w_out.astype(BF16)
    wu = w_up.astype(BF16)
    wd = w_down.astype(BF16)
    a_width = 3 * A_GROUP_WIDTH
    b_width = B_Q_WIDTH + 2 * B_KV_WIDTH

    vec_a = tile(jnp.concatenate([tile(a_q_norm * q_scale, (1, A_HEADS_PER_GROUP)),
                                  tile(a_k_norm, (1, A_HEADS_PER_GROUP)),
                                  jnp.ones((depth, A_GROUP_WIDTH), F32)], axis=1), (1, A_GROUPS))
    vec_b = jnp.concatenate([tile(b_q_norm * q_scale, (1, B_Q_HEADS)), tile(b_k_norm, (1, B_KV_HEADS)),
                             jnp.ones((depth, B_KV_WIDTH), F32)], axis=1)
    vec_c = jnp.concatenate([jnp.ones((C_WIDTH,), F32), jnp.full((C_WIDTH,), C_HEAD_DIM ** -0.5, F32),
                             jnp.ones((2 * C_WIDTH,), F32)])
    half = B_PAIRS // 2
    sinks = (b_sinks.astype(F32) * LOG2E).reshape(depth, 2, 2, half).transpose(0, 1, 3, 2).reshape(depth, B_Q_HEADS)
    assert np.arange(B_Q_HEADS).reshape(2, 2, half).transpose(0, 2, 1).reshape(-1).tolist() == b_order

    u = _rmsnorm(x, mix_norm[0])
    for l in range(depth):
        kinds = ["qk"] * (2 * A_PAIRS) + ["plain"] * A_PAIRS
        qkv = _proj(u, w_a, l, 0, [(kinds, dil) for _, dil in A_DILATED], tm=A_SUPER, tn=a_width,
                    vec=vec_a[l], tabs=rope, bd=bd)
        o_a = _attn_a(qkv)

        kinds = ["qk"] * ((B_Q_WIDTH + B_KV_WIDTH) // LANES) + ["plain"] * (B_KV_WIDTH // LANES)
        bqkv = _proj(u, w_b, l, 0, [(kinds, 1)], tm=1024, tn=b_width, vec=vec_b[l], tabs=rope, bd=bd)
        o_b = _attn_b(bqkv, sinks[l])

        heads = C_WIDTH // LANES
        c_all = _proj(u, w_tail, l, tail_col(6), [(["rot"] * (2 * heads), 1), (["plain"] * heads + ["silu"] * heads, 1)],
                      tm=1024, tn=2 * C_WIDTH, vec=vec_c, tabs=rot_c, weight_stationary=True)
        o_c = _retention(c_all, c_gn[l], decay_tabs)

        merged = _branch(u, w_tail, [tail_col(seg) for seg in (10, 11, 12)], o_a, o_b, o_c, wa, wb, wc, l)
        x, u2 = _outproj(merged, wo, l, x, mlp_norm[l])
        if l + 1 < depth:
            x, u = _mlp(u2, wu, wd, l, x, next_gain=mix_norm[l + 1])
        else:
            x = _mlp(u2, wu, wd, l, x)
    return x.reshape(b, s, d)
```
